```python
import math
import jax
import jax.numpy as jnp
from jax import lax
import numpy as np

D_MODEL = 1024
BATCH = 8
SEQ = 4096
DEPTH = 2

N_MIXERS = 2
N_SSD_LAYERS = (DEPTH + N_MIXERS - 1) // N_MIXERS
N_MLA_LAYERS = DEPTH // N_MIXERS
N_SUBLAYERS = 3

D_FF = 2816
FFN_RES_WEIGHT = 0.5
NORM_EPS = 1e-6

SSD_EXPAND = 2
D_INNER = SSD_EXPAND * D_MODEL
SSD_HEAD_DIM = 64
SSD_HEADS = D_INNER // SSD_HEAD_DIM
SSD_GROUPS = 8
SSD_HEADS_PER_GROUP = SSD_HEADS // SSD_GROUPS
SSD_STATE = 128
SSD_CONV = 5
SSD_CHUNK = 128
SSD_CONV_CH = D_INNER + 2 * SSD_GROUPS * SSD_STATE
SSD_IN = D_INNER + SSD_CONV_CH + 2 * SSD_HEADS
SSD_NORM_GROUP = D_INNER // SSD_GROUPS

MLA_HEADS = 16
MLA_Q_LORA = 384
MLA_KV_LORA = 256
MLA_NOPE = 64
MLA_ROPE = 32
MLA_V = 64
MLA_QK = MLA_NOPE + MLA_ROPE
MLA_IN = MLA_Q_LORA + MLA_KV_LORA + MLA_ROPE
ROPE_THETA = 10000.0
ATTN_Q_BLOCK = 128
MAX_POS_OFFSET = 1024

kernel_name = "hybrid_ssd_mla_macaron_adaln_encoder"


def rms_norm(x, g):
    xf = x.astype(jnp.float32)
    y = xf * lax.rsqrt(jnp.mean(xf * xf, axis=-1, keepdims=True) + NORM_EPS)
    return (y * g.astype(jnp.float32)).astype(x.dtype)


def modulate(x, g, shift, scale):
    return rms_norm(x, g) * (1.0 + scale[:, None, :]) + shift[:, None, :]


def swiglu(h, w_gate, w_up, w_down):
    return (jax.nn.silu(h @ w_gate) * (h @ w_up)) @ w_down


def depthwise_conv_centred(x, w, bias):
    ch = x.shape[-1]
    pad = SSD_CONV // 2
    out = lax.conv_general_dilated(
        x, w[:, None, :], window_strides=(1,), padding=[(pad, pad)],
        dimension_numbers=("NWC", "WIO", "NWC"), feature_group_count=ch)
    return out + bias


def ssd_chunked(x, dt, A, Bm, Cm):
    b, s, g, r, p = x.shape
    n = Bm.shape[-1]
    nc = s // SSD_CHUNK
    q = SSD_CHUNK
    x = x.reshape(b, nc, q, g, r, p)
    dt = dt.reshape(b, nc, q, g, r)
    Bm = Bm.reshape(b, nc, q, g, n)
    Cm = Cm.reshape(b, nc, q, g, n)
    a_cum = jnp.cumsum(dt * A, axis=2)
    xdt = x * dt[..., None]

    seg = a_cum[:, :, :, None] - a_cum[:, :, None, :]
    mask = jnp.tril(jnp.ones((q, q), dtype=bool))[:, :, None, None]
    decay = jnp.exp(jnp.where(mask, seg, -jnp.inf))
    cb = jnp.einsum("bcign,bcjgn->bcijg", Cm, Bm)
    y_diag = jnp.einsum("bcijgr,bcjgrp->bcigrp", cb[..., None] * decay, xdt)

    decay_to_end = jnp.exp(a_cum[:, :, -1:] - a_cum)
    states = jnp.einsum("bcjgn,bcjgrp->bcgrpn", Bm, xdt * decay_to_end[..., None])
    chunk_decay = jnp.exp(a_cum[:, :, -1])

    def step(h, inp):
        st, dec = inp
        return h * dec[..., None, None] + st, h

    h0 = jnp.zeros((b, g, r, p, n), dtype=states.dtype)
    _, h_prev = lax.scan(step, h0, (jnp.moveaxis(states, 1, 0), jnp.moveaxis(chunk_decay, 1, 0)))
    h_prev = jnp.moveaxis(h_prev, 0, 1)

    y_off = jnp.einsum("bcign,bcgrpn->bcigrp", Cm, h_prev) * jnp.exp(a_cum)[..., None]
    return (y_diag + y_off).reshape(b, s, g, r, p)


def ssd_mixer(h, w_in, conv_w, conv_b, dt_bias, a_log, d_skip, norm_g, w_out):
    b, s, _ = h.shape
    g, r = SSD_GROUPS, SSD_HEADS_PER_GROUP
    proj = h @ w_in
    z, xbc, dt_raw = jnp.split(proj, [D_INNER, D_INNER + SSD_CONV_CH], axis=-1)
    xbc = jax.nn.silu(depthwise_conv_centred(xbc, conv_w, conv_b))
    xs, Bm, Cm = jnp.split(xbc, [D_INNER, D_INNER + SSD_GROUPS * SSD_STATE], axis=-1)
    xs = xs.reshape(b, s, g, r, SSD_HEAD_DIM)
    Bm = Bm.reshape(b, s, g, SSD_STATE)
    Cm = Cm.reshape(b, s, g, SSD_STATE)
    dt = jax.nn.softplus(dt_raw.reshape(b, s, 2, g, r) + dt_bias.reshape(2, g, r))
    A = -jnp.exp(a_log).reshape(2, g, r)

    flip = lambda t: jnp.flip(t, axis=1)
    y_fwd = ssd_chunked(xs, dt[:, :, 0], A[0], Bm, Cm)
    y_bwd = flip(ssd_chunked(flip(xs), flip(dt[:, :, 1]), A[1], flip(Bm), flip(Cm)))
    y = y_fwd + y_bwd + xs * d_skip.reshape(g, r)[..., None]

    y = y.reshape(b, s, D_INNER) * jax.nn.silu(z)
    y = rms_norm(y.reshape(b, s, g, SSD_NORM_GROUP), norm_g.reshape(g, SSD_NORM_GROUP))
    return y.reshape(b, s, D_INNER) @ w_out


def rope_tables(positions):
    inv_freq = ROPE_THETA ** (-jnp.arange(0, MLA_ROPE, 2, dtype=jnp.float32) / MLA_ROPE)
    ang = positions.astype(jnp.float32)[..., None] * inv_freq
    return jnp.cos(ang)[:, :, None, :], jnp.sin(ang)[:, :, None, :]


def apply_rope_tail(t, cos, sin):
    t_nope, t_pe = t[..., :MLA_NOPE], t[..., MLA_NOPE:]
    x1, x2 = jnp.split(t_pe.astype(jnp.float32), 2, axis=-1)
    rot = jnp.concatenate([x1 * cos - x2 * sin, x2 * cos + x1 * sin], axis=-1).astype(t.dtype)
    return jnp.concatenate([t_nope, rot], axis=-1)


def blocked_attention(q, k, v):
    b, s, h, dq = q.shape
    nq = s // ATTN_Q_BLOCK
    scale = dq ** -0.5
    qb = jnp.moveaxis(q.reshape(b, nq, ATTN_Q_BLOCK, h, dq), 1, 0)

    def one_block(q_blk):
        logits = jnp.einsum("bqhd,bkhd->bhqk", q_blk, k).astype(jnp.float32) * scale
        p = jax.nn.softmax(logits, axis=-1).astype(v.dtype)
        return jnp.einsum("bhqk,bkhv->bqhv", p, v)

    o = lax.map(one_block, qb)
    return jnp.moveaxis(o, 0, 1).reshape(b, s, h, v.shape[-1])


def mla_mixer(h, positions, w_in, q_norm_g, kv_norm_g, w_uq, w_ukv, q_head_g, k_head_g, w_out):
    b, s, _ = h.shape
    q_lat, kv_lat, k_pe = jnp.split(h @ w_in, [MLA_Q_LORA, MLA_Q_LORA + MLA_KV_LORA], axis=-1)
    q = (rms_norm(q_lat, q_norm_g) @ w_uq).reshape(b, s, MLA_HEADS, MLA_QK)
    kv = (rms_norm(kv_lat, kv_norm_g) @ w_ukv).reshape(b, s, MLA_HEADS, MLA_NOPE + MLA_V)
    k_nope, v = jnp.split(kv, [MLA_NOPE], axis=-1)
    k_pe = jnp.broadcast_to(k_pe[:, :, None, :], (b, s, MLA_HEADS, MLA_ROPE))
    k = jnp.concatenate([k_nope, k_pe], axis=-1)
    q = rms_norm(q, q_head_g)
    k = rms_norm(k, k_head_g)
    cos, sin = rope_tables(positions)
    q = apply_rope_tail(q, cos, sin)
    k = apply_rope_tail(k, cos, sin)
    o = blocked_attention(q, k, v)
    return o.reshape(b, s, MLA_HEADS * MLA_V) @ w_out


def setup_inputs(seed: int = 0) -> dict:
    key = jax.random.key(seed)
    ks = iter(jax.random.split(key, 40))

    def normal(shape, scale):
        return scale * jax.random.normal(next(ks), shape, jnp.float32)

    def gain(shape):
        return 1.0 + 0.05 * jax.random.normal(next(ks), shape, jnp.float32)

    na, nb = N_SSD_LAYERS, N_MLA_LAYERS
    x = normal((BATCH, SEQ, D_MODEL), 1.0)
    c = normal((BATCH, D_MODEL), 1.0)
    positions = (jax.random.randint(next(ks), (BATCH, 1), 0, MAX_POS_OFFSET, dtype=jnp.int32)
                 + jnp.arange(SEQ, dtype=jnp.int32)[None, :])
    norm_g = gain((DEPTH, N_SUBLAYERS, D_MODEL))
    w_mod = normal((DEPTH, D_MODEL, N_SUBLAYERS * 3 * D_MODEL), D_MODEL ** -0.5)
    b_mod = normal((DEPTH, N_SUBLAYERS * 3 * D_MODEL), 0.02)
    ffn_w_gate = normal((DEPTH, 2, D_MODEL, D_FF), D_MODEL ** -0.5)
    ffn_w_up = normal((DEPTH, 2, D_MODEL, D_FF), D_MODEL ** -0.5)
    ffn_w_down = normal((DEPTH, 2, D_FF, D_MODEL), D_FF ** -0.5)

    ssd_w_in = normal((na, D_MODEL, SSD_IN), D_MODEL ** -0.5)
    ssd_conv_w = normal((na, SSD_CONV, SSD_CONV_CH), SSD_CONV ** -0.5)
    ssd_conv_b = normal((na, SSD_CONV_CH), 0.02)
    dt0 = jnp.exp(jax.random.uniform(next(ks), (na, 2, SSD_HEADS), jnp.float32,
                                     math.log(1e-3), math.log(1e-1)))
    ssd_dt_bias = dt0 + jnp.log(-jnp.expm1(-dt0))
    ssd_a_log = jnp.log(jax.random.uniform(next(ks), (na, 2, SSD_HEADS), jnp.float32, 1.0, 16.0))
    ssd_d = gain((na, SSD_HEADS))
    ssd_norm_g = gain((na, D_INNER))
    ssd_w_out = normal((na, D_INNER, D_MODEL), D_INNER ** -0.5)

    mla_w_in = normal((nb, D_MODEL, MLA_IN), D_MODEL ** -0.5)
    mla_q_norm_g = gain((nb, MLA_Q_LORA))
    mla_kv_norm_g = gain((nb, MLA_KV_LORA))
    mla_w_uq = normal((nb, MLA_Q_LORA, MLA_HEADS * MLA_QK), MLA_Q_LORA ** -0.5)
    mla_w_ukv = normal((nb, MLA_KV_LORA, MLA_HEADS * (MLA_NOPE + MLA_V)), MLA_KV_LORA ** -0.5)
    mla_q_head_g = gain((nb, MLA_QK))
    mla_k_head_g = gain((nb, MLA_QK))
    mla_w_out = normal((nb, MLA_HEADS * MLA_V, D_MODEL), (MLA_HEADS * MLA_V) ** -0.5)

    return {
        "x": x, "c": c, "positions": positions,
        "norm_g": norm_g, "w_mod": w_mod, "b_mod": b_mod,
        "ffn_w_gate": ffn_w_gate, "ffn_w_up": ffn_w_up, "ffn_w_down": ffn_w_down,
        "ssd_w_in": ssd_w_in, "ssd_conv_w": ssd_conv_w, "ssd_conv_b": ssd_conv_b,
        "ssd_dt_bias": ssd_dt_bias, "ssd_a_log": ssd_a_log, "ssd_d": ssd_d,
        "ssd_norm_g": ssd_norm_g, "ssd_w_out": ssd_w_out,
        "mla_w_in": mla_w_in, "mla_q_norm_g": mla_q_norm_g, "mla_kv_norm_g": mla_kv_norm_g,
        "mla_w_uq": mla_w_uq, "mla_w_ukv": mla_w_ukv, "mla_q_head_g": mla_q_head_g,
        "mla_k_head_g": mla_k_head_g, "mla_w_out": mla_w_out,
    }


def reference(x, c, positions, norm_g, w_mod, b_mod, ffn_w_gate, ffn_w_up, ffn_w_down,
              ssd_w_in, ssd_conv_w, ssd_conv_b, ssd_dt_bias, ssd_a_log, ssd_d,
              ssd_norm_g, ssd_w_out,
              mla_w_in, mla_q_norm_g, mla_kv_norm_g, mla_w_uq, mla_w_ukv,
              mla_q_head_g, mla_k_head_g, mla_w_out):
    cond = jax.nn.silu(c)
    for i in range(DEPTH):
        mod = (cond @ w_mod[i] + b_mod[i]).reshape(-1, N_SUBLAYERS, 3, D_MODEL)

        h = modulate(x, norm_g[i, 0], mod[:, 0, 0], mod[:, 0, 1])
        x = x + FFN_RES_WEIGHT * mod[:, 0, 2][:, None, :] * swiglu(
            h, ffn_w_gate[i, 0], ffn_w_up[i, 0], ffn_w_down[i, 0])

        h = modulate(x, norm_g[i, 1], mod[:, 1, 0], mod[:, 1, 1])
        j = i // N_MIXERS
        if i % N_MIXERS == 0:
            y = ssd_mixer(h, ssd_w_in[j], ssd_conv_w[j], ssd_conv_b[j], ssd_dt_bias[j],
                          ssd_a_log[j], ssd_d[j], ssd_norm_g[j], ssd_w_out[j])
        else:
            y = mla_mixer(h, positions, mla_w_in[j], mla_q_norm_g[j], mla_kv_norm_g[j],
                          mla_w_uq[j], mla_w_ukv[j], mla_q_head_g[j], mla_k_head_g[j],
                          mla_w_out[j])
        x = x + mod[:, 1, 2][:, None, :] * y

        h = modulate(x, norm_g[i, 2], mod[:, 2, 0], mod[:, 2, 1])
        x = x + FFN_RES_WEIGHT * mod[:, 2, 2][:, None, :] * swiglu(
            h, ffn_w_gate[i, 1], ffn_w_up[i, 1], ffn_w_down[i, 1])
    return x
```

```python
import functools

import jax
import jax.numpy as jnp
from jax import lax
from jax.experimental import pallas as pl
from jax.experimental.pallas import tpu as pltpu

F32 = jnp.float32
BF16 = jnp.bfloat16

N_SUBLAYERS = 3
FFN_RES_WEIGHT = 0.5
NORM_EPS = 1e-6
SSD_HEAD_DIM = 64
SSD_GROUPS = 8
SSD_HEADS_PER_GROUP = 4
SSD_STATE = 128
SSD_CONV = 5
SSD_CHUNK = 128
MLA_HEADS = 16
MLA_Q_LORA = 384
MLA_KV_LORA = 256
MLA_NOPE = 64
MLA_ROPE = 32
MLA_V = 64
MLA_QK = MLA_NOPE + MLA_ROPE
ROPE_THETA = 10000.0

LANES = 128
BF16_SUBLANES = 16
VMEM_LIMIT_BYTES = 56 * 1024 * 1024

ROW_TILE = 512
ATTN_Q_TILE = 512
MLA_HEAD_LANES = LANES
GROUP_LANES = SSD_HEADS_PER_GROUP * SSD_HEAD_DIM


def _params(*semantics):
    return pltpu.CompilerParams(dimension_semantics=semantics,
                                vmem_limit_bytes=VMEM_LIMIT_BYTES)


def _silu(v):
    return v * jax.nn.sigmoid(v)


def _rms(v, eps_mean_count=None):
    ms = jnp.mean(v * v, axis=-1, keepdims=True)
    return v * lax.rsqrt(ms + NORM_EPS)


def _modulated(x, g, mod_ref, sub):
    shift = mod_ref[3 * sub:3 * sub + 1, :]
    scale = mod_ref[3 * sub + 1:3 * sub + 2, :]
    return _rms(x) * g * (1.0 + scale) + shift


def _dot(a, b):
    return jnp.dot(a, b, preferred_element_type=F32)


def _dot_nt(a, b):
    return lax.dot_general(a, b, (((1,), (1,)), ((), ())), preferred_element_type=F32)


def _mod_kernel(c_ref, w_ref, b_ref, o_ref):
    o_ref[...] = _dot(_silu(c_ref[...]), w_ref[...]) + b_ref[...]


def _mod_call(c, w_mod, b_mod):
    depth, d, n = w_mod.shape
    bsz = c.shape[0]
    tn = n // 8
    return pl.pallas_call(
        _mod_kernel,
        grid=(depth, n // tn),
        in_specs=[
            pl.BlockSpec((bsz, d), lambda i, j: (0, 0)),
            pl.BlockSpec((None, d, tn), lambda i, j: (i, 0, j)),
            pl.BlockSpec((None, 1, tn), lambda i, j: (i, 0, j)),
        ],
        out_specs=pl.BlockSpec((None, bsz, tn), lambda i, j: (i, 0, j)),
        out_shape=jax.ShapeDtypeStruct((depth, bsz, n), F32),
        compiler_params=_params("arbitrary", "arbitrary"),
        name="mod",
    )(c, w_mod, b_mod.reshape(depth, 1, n))


def _ff_chunks(d_ff):
    half = (d_ff // 512) * 256
    return ((0, half), (half, d_ff)) if 0 < half < d_ff else ((0, d_ff),)


def _ffn_apply(x, mod_ref, g_ref, wg_ref, wu_ref, wd_ref, sub):
    h = _modulated(x, g_ref[...], mod_ref, sub).astype(BF16)
    acc = None
    for lo, hi in _ff_chunks(wg_ref.shape[1]):
        gate = _dot(h, wg_ref[:, lo:hi])
        up = _dot(h, wu_ref[:, lo:hi])
        part = _dot((_silu(gate) * up).astype(BF16), wd_ref[lo:hi, :])
        acc = part if acc is None else acc + part
    res_gate = mod_ref[3 * sub + 2:3 * sub + 3, :]
    return x + (FFN_RES_WEIGHT * res_gate) * acc


def _ffn_kernel(x_ref, mod_ref, g_ref, wg_ref, wu_ref, wd_ref, o_ref, *, sub):
    o_ref[...] = _ffn_apply(x_ref[...], mod_ref, g_ref, wg_ref, wu_ref, wd_ref, sub)


def _proj_ffn_kernel(x_ref, y_ref, wo_ref, mod_ref, g_ref, wg_ref, wu_ref, wd_ref, o_ref):
    mix_gate = mod_ref[5:6, :]
    x = x_ref[...] + mix_gate * _dot(y_ref[...], wo_ref[...])
    o_ref[...] = _ffn_apply(x, mod_ref, g_ref, wg_ref, wu_ref, wd_ref, 2)


def _resident(shape):
    return pl.BlockSpec(shape, lambda *_: (0,) * len(shape), pipeline_mode=pl.Buffered(1))


def _row_specs(t, d, seq, tm):
    tiles_per_seq = seq // tm
    x_spec = pl.BlockSpec((tm, d), lambda i: (i, 0))
    mod_spec = pl.BlockSpec((None, 3 * N_SUBLAYERS, d), lambda i: (i // tiles_per_seq, 0, 0))
    return x_spec, mod_spec


def _ffn_call(x, mod, g, wg, wu, wd, *, sub, seq):
    t, d = x.shape
    tm = ROW_TILE
    x_spec, mod_spec = _row_specs(t, d, seq, tm)
    return pl.pallas_call(
        functools.partial(_ffn_kernel, sub=sub),
        grid=(t // tm,),
        in_specs=[x_spec, mod_spec, _resident((1, d)), _resident(wg.shape), _resident(wu.shape),
                  _resident(wd.shape)],
        out_specs=x_spec,
        out_shape=jax.ShapeDtypeStruct((t, d), F32),
        compiler_params=_params("arbitrary"),
        name="ffn",
    )(x, mod, g, wg, wu, wd)


def _proj_ffn_call(x, y, wo, mod, g, wg, wu, wd, *, seq):
    t, d = x.shape
    tm = ROW_TILE
    x_spec, mod_spec = _row_specs(t, d, seq, tm)
    y_spec = pl.BlockSpec((tm, y.shape[1]), lambda i: (i, 0))
    return pl.pallas_call(
        _proj_ffn_kernel,
        grid=(t // tm,),
        in_specs=[x_spec, y_spec, _resident(wo.shape), mod_spec, _resident((1, d)), _resident(wg.shape),
                  _resident(wu.shape), _resident(wd.shape)],
        out_specs=x_spec,
        out_shape=jax.ShapeDtypeStruct((t, d), F32),
        compiler_params=_params("arbitrary"),
        name="proj_ffn",
    )(x, y, wo, mod, g, wg, wu, wd)


def _ssd_in_kernel(x_ref, mod_ref, g_ref, wz_ref, wxbc_ref, wdt_ref, z_ref, xbc_ref, dt_ref):
    h = _modulated(x_ref[...], g_ref[...], mod_ref, 1).astype(BF16)
    z_ref[...] = _dot(h, wz_ref[...]).astype(BF16)
    xbc_ref[...] = _dot(h, wxbc_ref[...]).astype(BF16)
    dt_ref[...] = _dot(h, wdt_ref[...])


def _ssd_in_call(x, mod, g, wz, wxbc, wdt, *, seq):
    t, d = x.shape
    tm = ROW_TILE
    x_spec, mod_spec = _row_specs(t, d, seq, tm)
    widths = (wz.shape[1], wxbc.shape[1], wdt.shape[1])
    return pl.pallas_call(
        _ssd_in_kernel,
        grid=(t // tm,),
        in_specs=[x_spec, mod_spec, _resident((1, d)), _resident(wz.shape), _resident(wxbc.shape),
                  _resident(wdt.shape)],
        out_specs=[pl.BlockSpec((tm, w), lambda i: (i, 0)) for w in widths],
        out_shape=[jax.ShapeDtypeStruct((t, widths[0]), BF16),
                   jax.ShapeDtypeStruct((t, widths[1]), BF16),
                   jax.ShapeDtypeStruct((t, widths[2]), F32)],
        compiler_params=_params("arbitrary"),
        name="ssd_in",
    )(x, mod, g, wz, wxbc, wdt)


def _ssd_core_kernel(xr_ref, br_ref, cr_ref, z_ref, dtr_ref, cwx_ref, cwb_ref, cwc_ref,
                     cbx_ref, cbb_ref, cbc_ref, dtb_ref, alog_ref, dsk_ref, ng_ref, o_ref,
                     xs_ref, bs_ref, cs_ref, bt_ref, yacc_ref, st_ref):
    seq = xr_ref.shape[0]
    q = SSD_CHUNK
    halo = BF16_SUBLANES
    nc = seq // q
    group = pl.program_id(1)

    def conv_chunk(c, carry):
        r0 = pl.multiple_of(c * q, q)
        rp = pl.multiple_of(jnp.maximum(r0 - halo, 0), halo)
        rn = pl.multiple_of(jnp.minimum(r0 + q, seq - halo), halo)

        def conv(src_ref, w_ref, b_ref):
            cur = src_ref[pl.ds(r0, q), :].astype(F32)
            prev = jnp.where(c > 0, src_ref[pl.ds(rp, halo), :].astype(F32), 0.0)
            nxt = jnp.where(c < nc - 1, src_ref[pl.ds(rn, halo), :].astype(F32), 0.0)
            win = jnp.concatenate([prev, cur, nxt], axis=0)
            first = halo - SSD_CONV // 2
            acc = b_ref[...] + w_ref[0:1, :] * win[first:first + q]
            for k in range(1, SSD_CONV):
                acc = acc + w_ref[k:k + 1, :] * win[first + k:first + k + q]
            return _silu(acc)

        xs_ref[pl.ds(r0, q), :] = conv(xr_ref, cwx_ref, cbx_ref)
        bc = conv(br_ref, cwb_ref, cbb_ref)
        bs_ref[pl.ds(r0, q), :] = bc.astype(BF16)
        bt_ref[c] = bc.T.astype(BF16)
        cs_ref[pl.ds(r0, q), :] = conv(cr_ref, cwc_ref, cbc_ref).astype(BF16)
        return carry

    lax.fori_loop(0, nc, conv_chunk, 0)

    dt_shift = jnp.where(group == 0, 0, LANES - 2 * SSD_HEADS_PER_GROUP * group)
    head_id = lax.broadcasted_iota(jnp.int32, (q, GROUP_LANES), 1) // SSD_HEAD_DIM
    row_i = lax.broadcasted_iota(jnp.int32, (q, q), 0)
    col_j = lax.broadcasted_iota(jnp.int32, (q, q), 1)

    def expand_heads(cols, first):
        out = jnp.broadcast_to(cols[:, first + 3:first + 4], (q, GROUP_LANES))
        for r in (2, 1, 0):
            out = jnp.where(head_id == r, cols[:, first + r:first + r + 1], out)
        return out

    def scan(direction):
        tri = (row_i >= col_j) if direction == 0 else (col_j >= row_i)
        tri_bf = jnp.where(tri, 1.0, 0.0).astype(BF16)
        a_coef = -jnp.exp(alog_ref[direction:direction + 1, :])
        st_ref[...] = jnp.zeros_like(st_ref)

        def chunk(t, carry):
            c = t if direction == 0 else nc - 1 - t
            r0 = pl.multiple_of(c * q, q)
            x = xs_ref[pl.ds(r0, q), :]
            b_c = bs_ref[pl.ds(r0, q), :]
            c_c = cs_ref[pl.ds(r0, q), :]
            dt_cols = jax.nn.softplus(
                pltpu.roll(dtr_ref[pl.ds(r0, q), :] + dtb_ref[...], dt_shift, 1))
            dt = expand_heads(dt_cols, SSD_HEADS_PER_GROUP * direction)
            a = dt * a_coef
            a1 = a.astype(BF16)
            rem = a - a1.astype(F32)
            a2 = rem.astype(BF16)
            a3 = (rem - a2.astype(F32)).astype(BF16)
            cum = _dot(tri_bf, a1) + _dot(tri_bf, a2) + _dot(tri_bf, a3)
            total = cum[q - 1:q, :] if direction == 0 else cum[0:1, :]
            xdt = x * dt
            xdt_bf = xdt.astype(BF16)
            cb = _dot_nt(c_c, b_c)
            cum_t = (cum[:, :LANES].T, cum[:, LANES:].T)
            y = None
            for r in range(SSD_HEADS_PER_GROUP):
                lane = r * SSD_HEAD_DIM
                col = cum[:, lane:lane + 1]
                row = cum_t[lane // LANES][lane % LANES:lane % LANES + 1, :]
                decay = jnp.exp(jnp.where(tri, col - row, -jnp.inf))
                y_r = _dot((cb * decay).astype(BF16), xdt_bf)
                y = y_r if y is None else jnp.where(head_id == r, y_r, y)
            h_in = st_ref[...]
            y = y + _dot(c_c, h_in.astype(BF16)) * jnp.exp(cum)
            st_ref[...] = h_in * jnp.exp(total) + _dot(
                bt_ref[c], (xdt * jnp.exp(total - cum)).astype(BF16))
            if direction == 0:
                yacc_ref[pl.ds(r0, q), :] = y
            else:
                y = yacc_ref[pl.ds(r0, q), :] + y + x * dsk_ref[...]
                y = y * _silu(z_ref[pl.ds(r0, q), :].astype(F32))
                o_ref[pl.ds(r0, q), :] = (_rms(y) * ng_ref[...]).astype(BF16)
            return carry

        lax.fori_loop(0, nc, chunk, 0)

    scan(0)
    scan(1)


def _ssd_core_call(z, xbc, dt_raw, conv_w, conv_b, dt_bias, a_log_exp, d_skip_exp, norm_g, *, batch, seq):
    t, d_inner = z.shape
    n = SSD_STATE
    gl = GROUP_LANES
    x_blocks = d_inner // n
    c_first = x_blocks + SSD_GROUPS
    nc = seq // SSD_CHUNK

    def col(width, first):
        return lambda b, g: (b, first + g)

    def par(rows, width, first):
        return pl.BlockSpec((rows, width), lambda b, g: (0, first + g))

    in_specs = [
        pl.BlockSpec((seq, gl), lambda b, g: (b, g)),
        pl.BlockSpec((seq, n), lambda b, g: (b, x_blocks + g)),
        pl.BlockSpec((seq, n), lambda b, g: (b, c_first + g)),
        pl.BlockSpec((seq, gl), lambda b, g: (b, g)),
        pl.BlockSpec((seq, LANES), lambda b, g: (b, 0)),
        par(SSD_CONV, gl, 0), par(SSD_CONV, n, x_blocks), par(SSD_CONV, n, c_first),
        par(1, gl, 0), par(1, n, x_blocks), par(1, n, c_first),
        pl.BlockSpec((1, LANES), lambda b, g: (0, 0)),
        par(2, gl, 0), par(1, gl, 0), par(1, gl, 0),
    ]
    return pl.pallas_call(
        _ssd_core_kernel,
        grid=(batch, SSD_GROUPS),
        in_specs=in_specs,
        out_specs=pl.BlockSpec((seq, gl), lambda b, g: (b, g)),
        out_shape=jax.ShapeDtypeStruct((t, d_inner), BF16),
        scratch_shapes=[
            pltpu.VMEM((seq, gl), F32),
            pltpu.VMEM((seq, n), BF16),
            pltpu.VMEM((seq, n), BF16),
            pltpu.VMEM((nc, n, SSD_CHUNK), BF16),
            pltpu.VMEM((seq, gl), F32),
            pltpu.VMEM((n, gl), F32),
        ],
        compiler_params=_params("arbitrary", "arbitrary"),
        name="ssd_core",
    )(xbc, xbc, xbc, z, dt_raw, conv_w, conv_w, conv_w, conv_b, conv_b, conv_b,
      dt_bias, a_log_exp, d_skip_exp, norm_g)


def _mla_in_kernel(x_ref, pos_ref, mod_ref, g_ref, win_ref, qng_ref, kvng_ref, wuq_ref, wk_ref, wv_ref,
                   qhg_ref, khg_ref, freq_ref, sign_ref, q_ref, k_ref, v_ref):
    hl = MLA_HEAD_LANES
    h = _modulated(x_ref[...], g_ref[...], mod_ref, 1).astype(BF16)
    lat = _dot(h, win_ref[...])
    q_lat = (_rms(lat[:, :MLA_Q_LORA]) * qng_ref[...]).astype(BF16)
    kv_lat = (_rms(lat[:, MLA_Q_LORA:MLA_Q_LORA + MLA_KV_LORA]) * kvng_ref[...]).astype(BF16)
    k_pe = lat[:, MLA_Q_LORA + MLA_KV_LORA:]
    qf = _dot(q_lat, wuq_ref[...])
    kf = _dot(kv_lat, wk_ref[...])
    v_ref[...] = _dot(kv_lat, wv_ref[...]).astype(BF16)

    ang = pos_ref[...] * freq_ref[...]
    cos = jnp.cos(ang)
    sin = jnp.sin(ang) * sign_ref[...]
    q_scale = MLA_QK ** -0.5

    def head_norm_rope(t, gain):
        ms = jnp.sum(t * t, axis=-1, keepdims=True) * (1.0 / MLA_QK)
        t = t * lax.rsqrt(ms + NORM_EPS) * gain
        return t * cos + pltpu.roll(t, hl // 2, 1) * sin

    for hd in range(MLA_HEADS):
        sl = slice(hd * hl, (hd + 1) * hl)
        q_ref[:, sl] = (head_norm_rope(qf[:, sl], qhg_ref[...]) * q_scale).astype(BF16)
        k_ref[:, sl] = head_norm_rope(kf[:, sl] + k_pe, khg_ref[...]).astype(BF16)


def _mla_in_call(x, pos, mod, g, win, qng, kvng, wuq, wk, wv, qhg, khg, freq, sign, *, seq):
    t, d = x.shape
    tm = ROW_TILE
    x_spec, mod_spec = _row_specs(t, d, seq, tm)
    qk_w = MLA_HEADS * MLA_HEAD_LANES
    v_w = MLA_HEADS * MLA_V
    small = [qng, kvng]
    return pl.pallas_call(
        _mla_in_kernel,
        grid=(t // tm,),
        in_specs=[x_spec, pl.BlockSpec((tm, 1), lambda i: (i, 0)), mod_spec, _resident((1, d)),
                  _resident(win.shape), _resident(qng.shape), _resident(kvng.shape),
                  _resident(wuq.shape), _resident(wk.shape), _resident(wv.shape),
                  _resident(qhg.shape), _resident(khg.shape), _resident(freq.shape),
                  _resident(sign.shape)],
        out_specs=[pl.BlockSpec((tm, qk_w), lambda i: (i, 0)),
                   pl.BlockSpec((tm, qk_w), lambda i: (i, 0)),
                   pl.BlockSpec((tm, v_w), lambda i: (i, 0))],
        out_shape=[jax.ShapeDtypeStruct((t, qk_w), BF16),
                   jax.ShapeDtypeStruct((t, qk_w), BF16),
                   jax.ShapeDtypeStruct((t, v_w), BF16)],
        compiler_params=_params("arbitrary"),
        name="mla_in",
    )(x, pos, mod, g, win, qng, kvng, wuq, wk, wv, qhg, khg, freq, sign)


def _attn_kernel(q_ref, k_ref, v_ref, o_ref):
    hl = MLA_HEAD_LANES
    lane = lax.broadcasted_iota(jnp.int32, o_ref.shape, 1)
    out = None
    for hd in range(2):
        s = _dot_nt(q_ref[:, hd * hl:(hd + 1) * hl], k_ref[:, hd * hl:(hd + 1) * hl])
        p = jnp.exp(s - jnp.max(s, axis=-1, keepdims=True))
        denom = jnp.sum(p, axis=-1, keepdims=True)
        o = _dot(p.astype(BF16), v_ref[...]) / denom
        out = o if out is None else jnp.where(lane >= MLA_V, o, out)
    o_ref[...] = out.astype(BF16)


def _attn_call(q, k, v, *, batch, seq):
    tq = ATTN_Q_TILE
    nq = seq // tq
    pair_qk = 2 * MLA_HEAD_LANES
    pair_v = 2 * MLA_V
    return pl.pallas_call(
        _attn_kernel,
        grid=(batch, MLA_HEADS // 2, nq),
        in_specs=[pl.BlockSpec((tq, pair_qk), lambda b, p, i: (b * nq + i, p)),
                  pl.BlockSpec((seq, pair_qk), lambda b, p, i: (b, p)),
                  pl.BlockSpec((seq, pair_v), lambda b, p, i: (b, p))],
        out_specs=pl.BlockSpec((tq, pair_v), lambda b, p, i: (b * nq + i, p)),
        out_shape=jax.ShapeDtypeStruct(v.shape, BF16),
        compiler_params=_params("arbitrary", "arbitrary", "arbitrary"),
        name="attn",
    )(q, k, v)


def _mla_head_perm():
    half = MLA_ROPE // 2
    perm = [-1] * MLA_HEAD_LANES
    for l in range(half):
        perm[l] = MLA_NOPE + l
        perm[MLA_HEAD_LANES // 2 + l] = MLA_NOPE + half + l
    nope_lo = MLA_HEAD_LANES // 2 - half
    for l in range(nope_lo):
        perm[half + l] = l
    for l in range(MLA_NOPE - nope_lo):
        perm[MLA_HEAD_LANES // 2 + half + l] = nope_lo + l
    return perm


def _gather_cols(w, perm):
    idx = jnp.array([max(p, 0) for p in perm], jnp.int32)
    keep = jnp.array([p >= 0 for p in perm])
    return jnp.where(keep, jnp.take(w, idx, axis=-1), 0.0)


def _mla_layout(w_in, w_uq, w_ukv, q_head_g, k_head_g):
    perm = _mla_head_perm()
    half = MLA_ROPE // 2
    lat = MLA_Q_LORA + MLA_KV_LORA
    pe_perm = [p - MLA_NOPE if p >= MLA_NOPE else -1 for p in perm]
    win = jnp.concatenate([w_in[:, :lat], _gather_cols(w_in[:, lat:], pe_perm)], axis=1)
    wuq = _gather_cols(w_uq.reshape(MLA_Q_LORA, MLA_HEADS, MLA_QK), perm)
    wuq = wuq.reshape(MLA_Q_LORA, MLA_HEADS * MLA_HEAD_LANES)
    kv = w_ukv.reshape(MLA_KV_LORA, MLA_HEADS, MLA_NOPE + MLA_V)
    nope_perm = [p if 0 <= p < MLA_NOPE else -1 for p in perm]
    wk = _gather_cols(kv[:, :, :MLA_NOPE], nope_perm).reshape(MLA_KV_LORA, MLA_HEADS * MLA_HEAD_LANES)
    wv = kv[:, :, MLA_NOPE:].reshape(MLA_KV_LORA, MLA_HEADS * MLA_V)
    qhg = _gather_cols(q_head_g[None, :], perm)
    khg = _gather_cols(k_head_g[None, :], perm)
    inv_freq = ROPE_THETA ** (-jnp.arange(0, MLA_ROPE, 2, dtype=F32) / MLA_ROPE)
    zeros = jnp.zeros((MLA_HEAD_LANES // 2 - half,), F32)
    freq = jnp.concatenate([inv_freq, zeros, inv_freq, zeros])[None, :]
    sign = jnp.concatenate([-jnp.ones((half,), F32), zeros, jnp.ones((half,), F32), zeros])[None, :]
    return (win.astype(BF16), wuq.astype(BF16), wk.astype(BF16), wv.astype(BF16), qhg, khg, freq, sign)


def _ssd_layout(w_in, dt_bias, a_log, d_skip):
    d_inner = SSD_GROUPS * GROUP_LANES
    conv_ch = d_inner + 2 * SSD_GROUPS * SSD_STATE
    heads = SSD_GROUPS * SSD_HEADS_PER_GROUP
    wz = w_in[:, :d_inner].astype(BF16)
    wxbc = w_in[:, d_inner:d_inner + conv_ch].astype(BF16)

    def by_group(v):
        lead = v.shape[:-1]
        v = v.reshape(lead + (2, SSD_GROUPS, SSD_HEADS_PER_GROUP))
        v = jnp.swapaxes(v, -3, -2).reshape(lead + (2 * heads,))
        return jnp.pad(v, [(0, 0)] * len(lead) + [(0, LANES - 2 * heads)])

    wdt = by_group(w_in[:, d_inner + conv_ch:]).astype(BF16)
    dtb = by_group(dt_bias.reshape(1, 2 * heads))
    a_log_exp = jnp.repeat(a_log, SSD_HEAD_DIM, axis=-1)
    d_skip_exp = jnp.repeat(d_skip, SSD_HEAD_DIM)[None, :]
    return wz, wxbc, wdt, dtb, a_log_exp, d_skip_exp


def kernel(x, c, positions, norm_g, w_mod, b_mod, ffn_w_gate, ffn_w_up, ffn_w_down, ssd_w_in, ssd_conv_w,
           ssd_conv_b, ssd_dt_bias, ssd_a_log, ssd_d, ssd_norm_g, ssd_w_out, mla_w_in, mla_q_norm_g,
           mla_kv_norm_g, mla_w_uq, mla_w_ukv, mla_q_head_g, mla_k_head_g, mla_w_out):
    batch, seq, d = x.shape
    depth = w_mod.shape[0]
    t = batch * seq
    n_mixers = 2
    xf = x.reshape(t, d)
    pos = positions.astype(F32).reshape(t, 1)
    mod = _mod_call(c, w_mod, b_mod).reshape(depth, batch, 3 * N_SUBLAYERS, d)
    wg, wu, wd = (w.astype(BF16) for w in (ffn_w_gate, ffn_w_up, ffn_w_down))

    for i in range(depth):
        g = norm_g[i][:, None, :]
        xf = _ffn_call(xf, mod[i], g[0], wg[i, 0], wu[i, 0], wd[i, 0], sub=0, seq=seq)
        j = i // n_mixers
        if i % n_mixers == 0:
            wz, wxbc, wdt, dtb, a_log_exp, d_skip_exp = _ssd_layout(
                ssd_w_in[j], ssd_dt_bias[j], ssd_a_log[j], ssd_d[j])
            z, xbc, dt_raw = _ssd_in_call(xf, mod[i], g[1], wz, wxbc, wdt, seq=seq)
            y = _ssd_core_call(z, xbc, dt_raw, ssd_conv_w[j], ssd_conv_b[j][None, :], dtb, a_log_exp,
                               d_skip_exp, ssd_norm_g[j][None, :], batch=batch, seq=seq)
            w_out = ssd_w_out[j].astype(BF16)
        else:
            win, wuq, wk, wv, qhg, khg, freq, sign = _mla_layout(
                mla_w_in[j], mla_w_uq[j], mla_w_ukv[j], mla_q_head_g[j], mla_k_head_g[j])
            q, k, v = _mla_in_call(xf, pos, mod[i], g[1], win, mla_q_norm_g[j][None, :],
                                   mla_kv_norm_g[j][None, :], wuq, wk, wv, qhg, khg, freq, sign, seq=seq)
            y = _attn_call(q, k, v, batch=batch, seq=seq)
            w_out = mla_w_out[j].astype(BF16)
        xf = _proj_ffn_call(xf, y, w_out, mod[i], g[2], wg[i, 1], wu[i, 1], wd[i, 1], seq=seq)
    return xf.reshape(batch, seq, d)
```

```python
import functools

import jax
import jax.numpy as jnp
from jax import lax
from jax.experimental import pallas as pl
from jax.experimental.pallas import tpu as pltpu

F32 = jnp.float32
BF16 = jnp.bfloat16

N_SUBLAYERS = 3
FFN_RES_WEIGHT = 0.5
NORM_EPS = 1e-6
SSD_HEAD_DIM = 64
SSD_GROUPS = 8
SSD_HEADS_PER_GROUP = 4
SSD_STATE = 128
SSD_CONV = 5
SSD_CHUNK = 128
MLA_HEADS = 16
MLA_Q_LORA = 384
MLA_KV_LORA = 256
MLA_NOPE = 64
MLA_ROPE = 32
MLA_V = 64
MLA_QK = MLA_NOPE + MLA_ROPE
ROPE_THETA = 10000.0
LOG2_E = 1.4426950408889634

LANES = 128
BF16_SUBLANES = 16
VMEM_LIMIT_BYTES = 56 * 1024 * 1024

ROW_TILE = 512
ATTN_Q_TILE = 1024
ATTN_ROW_SPLIT = 4
MLA_HEAD_LANES = LANES
GROUP_LANES = SSD_HEADS_PER_GROUP * SSD_HEAD_DIM


def _params(*semantics):
    return pltpu.CompilerParams(dimension_semantics=semantics,
                                vmem_limit_bytes=VMEM_LIMIT_BYTES)


def _silu(v):
    return v * jax.nn.sigmoid(v)


def _rms(v):
    ms = jnp.mean(v * v, axis=-1, keepdims=True)
    return v * lax.rsqrt(ms + NORM_EPS)


def _modulated(x, g, mod_ref, sub):
    shift = mod_ref[3 * sub:3 * sub + 1, :]
    scale = mod_ref[3 * sub + 1:3 * sub + 2, :]
    return _rms(x) * g * (1.0 + scale) + shift


def _dot(a, b):
    return jnp.dot(a, b, preferred_element_type=F32)


def _dot_nt(a, b):
    return lax.dot_general(a, b, (((1,), (1,)), ((), ())), preferred_element_type=F32)


def _mod_kernel(c_ref, w_ref, b_ref, o_ref):
    o_ref[...] = _dot(_silu(c_ref[...]), w_ref[...]) + b_ref[...]


def _mod_call(c, w_mod, b_mod):
    depth, d, n = w_mod.shape
    bsz = c.shape[0]
    tn = n // 8
    return pl.pallas_call(
        _mod_kernel,
        grid=(depth, n // tn),
        in_specs=[
            pl.BlockSpec((bsz, d), lambda i, j: (0, 0)),
            pl.BlockSpec((None, d, tn), lambda i, j: (i, 0, j)),
            pl.BlockSpec((None, 1, tn), lambda i, j: (i, 0, j)),
        ],
        out_specs=pl.BlockSpec((None, bsz, tn), lambda i, j: (i, 0, j)),
        out_shape=jax.ShapeDtypeStruct((depth, bsz, n), F32),
        compiler_params=_params("arbitrary", "arbitrary"),
        name="mod",
    )(c, w_mod, b_mod.reshape(depth, 1, n))


def _ff_chunks(d_ff):
    half = (d_ff // 512) * 256
    return ((0, half), (half, d_ff)) if 0 < half < d_ff else ((0, d_ff),)


def _ffn_apply(x, mod_ref, g_ref, wg_ref, wu_ref, wd_ref, sub):
    h = _modulated(x, g_ref[...], mod_ref, sub).astype(BF16)
    acc = None
    for lo, hi in _ff_chunks(wg_ref.shape[1]):
        gate = _dot(h, wg_ref[:, lo:hi])
        up = _dot(h, wu_ref[:, lo:hi])
        part = _dot((_silu(gate) * up).astype(BF16), wd_ref[lo:hi, :])
        acc = part if acc is None else acc + part
    res_gate = mod_ref[3 * sub + 2:3 * sub + 3, :]
    return x + (FFN_RES_WEIGHT * res_gate) * acc


def _ffn_kernel(x_ref, mod_ref, g_ref, wg_ref, wu_ref, wd_ref, o_ref, *, sub):
    o_ref[...] = _ffn_apply(x_ref[...], mod_ref, g_ref, wg_ref, wu_ref, wd_ref, sub)


def _proj_ffn_kernel(x_ref, y_ref, wo_ref, mod_ref, g_ref, wg_ref, wu_ref, wd_ref, o_ref):
    mix_gate = mod_ref[5:6, :]
    x = x_ref[...] + mix_gate * _dot(y_ref[...], wo_ref[...])
    o_ref[...] = _ffn_apply(x, mod_ref, g_ref, wg_ref, wu_ref, wd_ref, 2)


def _resident(shape):
    return pl.BlockSpec(shape, lambda *_: (0,) * len(shape), pipeline_mode=pl.Buffered(1))


def _row_specs(t, d, seq, tm):
    tiles_per_seq = seq // tm
    x_spec = pl.BlockSpec((tm, d), lambda i: (i, 0))
    mod_spec = pl.BlockSpec((None, 3 * N_SUBLAYERS, d), lambda i: (i // tiles_per_seq, 0, 0))
    return x_spec, mod_spec


def _ffn_call(x, mod, g, wg, wu, wd, *, sub, seq):
    t, d = x.shape
    tm = ROW_TILE
    x_spec, mod_spec = _row_specs(t, d, seq, tm)
    return pl.pallas_call(
        functools.partial(_ffn_kernel, sub=sub),
        grid=(t // tm,),
        in_specs=[x_spec, mod_spec, _resident((1, d)), _resident(wg.shape), _resident(wu.shape),
                  _resident(wd.shape)],
        out_specs=x_spec,
        out_shape=jax.ShapeDtypeStruct((t, d), F32),
        compiler_params=_params("arbitrary"),
        name="ffn",
    )(x, mod, g, wg, wu, wd)


def _proj_ffn_call(x, y, wo, mod, g, wg, wu, wd, *, seq):
    t, d = x.shape
    tm = ROW_TILE
    x_spec, mod_spec = _row_specs(t, d, seq, tm)
    y_spec = pl.BlockSpec((tm, y.shape[1]), lambda i: (i, 0))
    return pl.pallas_call(
        _proj_ffn_kernel,
        grid=(t // tm,),
        in_specs=[x_spec, y_spec, _resident(wo.shape), mod_spec, _resident((1, d)), _resident(wg.shape),
                  _resident(wu.shape), _resident(wd.shape)],
        out_specs=x_spec,
        out_shape=jax.ShapeDtypeStruct((t, d), F32),
        compiler_params=_params("arbitrary"),
        name="proj_ffn",
    )(x, y, wo, mod, g, wg, wu, wd)


def _ssd_in_kernel(x_ref, mod_ref, g_ref, wz_ref, wxbc_ref, wdt_ref, z_ref, xbc_ref, dt_ref):
    h = _modulated(x_ref[...], g_ref[...], mod_ref, 1).astype(BF16)
    z_ref[...] = _dot(h, wz_ref[...]).astype(BF16)
    xbc_ref[...] = _dot(h, wxbc_ref[...]).astype(BF16)
    dt_ref[...] = _dot(h, wdt_ref[...])


def _ssd_in_call(x, mod, g, wz, wxbc, wdt, *, seq):
    t, d = x.shape
    tm = ROW_TILE
    x_spec, mod_spec = _row_specs(t, d, seq, tm)
    widths = (wz.shape[1], wxbc.shape[1], wdt.shape[1])
    return pl.pallas_call(
        _ssd_in_kernel,
        grid=(t // tm,),
        in_specs=[x_spec, mod_spec, _resident((1, d)), _resident(wz.shape), _resident(wxbc.shape),
                  _resident(wdt.shape)],
        out_specs=[pl.BlockSpec((tm, w), lambda i: (i, 0)) for w in widths],
        out_shape=[jax.ShapeDtypeStruct((t, widths[0]), BF16),
                   jax.ShapeDtypeStruct((t, widths[1]), BF16),
                   jax.ShapeDtypeStruct((t, widths[2]), F32)],
        compiler_params=_params("arbitrary"),
        name="ssd_in",
    )(x, mod, g, wz, wxbc, wdt)


def _split3(v):
    v1 = v.astype(BF16)
    rem = v - v1.astype(F32)
    v2 = rem.astype(BF16)
    return v1, v2, (rem - v2.astype(F32)).astype(BF16)


def _ssd_dt_kernel(dtr_ref, dtb_ref, alog_ref, cum_ref, dtt_ref, cumt_ref):
    seq = dtr_ref.shape[0]
    q = SSD_CHUNK
    row_i = lax.broadcasted_iota(jnp.int32, (q, q), 0)
    col_j = lax.broadcasted_iota(jnp.int32, (q, q), 1)
    tri = jnp.where(row_i >= col_j, 1.0, 0.0).astype(BF16)
    lane = lax.broadcasted_iota(jnp.int32, (q, LANES), 1)
    is_fwd = (lane // SSD_HEADS_PER_GROUP) % 2 == 0
    a_coef = -jnp.exp(alog_ref[...])

    def chunk(c, carry):
        r0 = pl.multiple_of(c * q, q)
        dt = jax.nn.softplus(dtr_ref[pl.ds(r0, q), :] + dtb_ref[...])
        a = dt * a_coef
        a1, a2, a3 = _split3(a)
        prefix = _dot(tri, a1) + _dot(tri, a2) + _dot(tri, a3)
        cum = jnp.where(is_fwd, prefix, prefix[q - 1:q, :] - prefix + a)
        cum_ref[pl.ds(r0, q), :] = cum
        dtt_ref[:, pl.ds(r0, q)] = dt.T
        cumt_ref[:, pl.ds(r0, q)] = cum.T
        return carry

    lax.fori_loop(0, seq // q, chunk, 0)


def _ssd_dt_call(dt_raw, dt_bias, a_log, *, batch, seq):
    t = dt_raw.shape[0]
    head_major = pl.BlockSpec((None, LANES, seq), lambda b: (b, 0, 0))
    return pl.pallas_call(
        _ssd_dt_kernel,
        grid=(batch,),
        in_specs=[pl.BlockSpec((seq, LANES), lambda b: (b, 0)),
                  pl.BlockSpec((1, LANES), lambda b: (0, 0)),
                  pl.BlockSpec((1, LANES), lambda b: (0, 0))],
        out_specs=[pl.BlockSpec((seq, LANES), lambda b: (b, 0)), head_major, head_major],
        out_shape=[jax.ShapeDtypeStruct((t, LANES), F32),
                   jax.ShapeDtypeStruct((batch, LANES, seq), F32),
                   jax.ShapeDtypeStruct((batch, LANES, seq), F32)],
        compiler_params=_params("arbitrary"),
        name="ssd_dt",
    )(dt_raw, dt_bias, a_log)


def _ssd_core_kernel(xr_ref, br_ref, cr_ref, z_ref, cum_ref, dtt_ref, cumt_ref, cw_ref, cb_ref,
                     dsk_ref, ng_ref, o_ref, xt_ref, cs_ref, yt_ref, st_ref):
    seq = xr_ref.shape[0]
    q = SSD_CHUNK
    halo = BF16_SUBLANES
    nc = seq // q
    gl = GROUP_LANES
    n = SSD_STATE
    hp = SSD_HEAD_DIM
    heads = SSD_HEADS_PER_GROUP
    centre = SSD_CONV // 2
    group = pl.program_id(1)

    col_shift = jnp.where(group == 0, 0, LANES - 2 * heads * group)
    src_j = lax.broadcasted_iota(jnp.int32, (q, q), 0)
    dst_i = lax.broadcasted_iota(jnp.int32, (q, q), 1)
    causal = (dst_i >= src_j, dst_i <= src_j)
    tap_row = lax.broadcasted_iota(jnp.int32, (q, q + 2 * halo), 0)
    tap_col = lax.broadcasted_iota(jnp.int32, (q, q + 2 * halo), 1)
    shifts = {k: jnp.where(tap_col == tap_row + (halo + k - centre), 1.0, 0.0).astype(BF16)
              for k in range(SSD_CONV) if k != centre}

    def rows_of(ref, r0, r1, r2):
        return jnp.concatenate([ref[pl.ds(r0, halo), :], ref[pl.ds(r1, q), :], ref[pl.ds(r2, halo), :]],
                               axis=0)

    def head_rows(vals, direction):
        return jnp.concatenate(
            [jnp.broadcast_to(vals[heads * direction + h:heads * direction + h + 1, :], (hp, vals.shape[1]))
             for h in range(heads)], axis=0)

    def phase_a(c, carry):
        r0 = pl.multiple_of(c * q, q)
        rp = pl.multiple_of(jnp.maximum(r0 - halo, 0), halo)
        rn = pl.multiple_of(jnp.minimum(r0 + q, seq - halo), halo)
        win = jnp.concatenate([rows_of(xr_ref, rp, r0, rn), rows_of(br_ref, rp, r0, rn),
                               rows_of(cr_ref, rp, r0, rn)], axis=1)
        row = lax.broadcasted_iota(jnp.int32, (q + 2 * halo, 1), 0)
        outside = ((row < halo) & (c == 0)) | ((row >= halo + q) & (c == nc - 1))
        win = jnp.where(outside, jnp.zeros_like(win), win)
        acc = cb_ref[...] + cw_ref[centre:centre + 1, :] * win[halo:halo + q].astype(F32)
        for k, shift in shifts.items():
            acc = acc + cw_ref[k:k + 1, :] * _dot(shift, win)
        act = _silu(acc)
        x = act[:, :gl]
        b_bf = act[:, gl:gl + n].astype(BF16)
        c_bf = act[:, gl + n:].astype(BF16)
        xt = jnp.concatenate([x[:, :LANES].T, x[:, LANES:].T], axis=0)
        xt_ref[c] = xt
        cs_ref[pl.ds(r0, q), :] = c_bf
        cbt = _dot_nt(b_bf, c_bf)
        cum_c = pltpu.roll(cum_ref[pl.ds(r0, q), :], col_shift, 1)
        dt_r = dtt_ref[:, pl.ds(r0, q)]
        cum_r = cumt_ref[:, pl.ds(r0, q)]
        yt = None
        for d in range(2):
            total = cum_r[:, q - 1:q] if d == 0 else cum_r[:, 0:1]
            to_edge = jnp.exp(total - cum_r)
            y_heads, w_heads = [], []
            for h in range(heads):
                k = heads * d + h
                xdt = xt[h * hp:(h + 1) * hp, :] * dt_r[k:k + 1, :]
                decay = jnp.exp(jnp.where(causal[d], cum_r[k:k + 1, :] - cum_c[:, k:k + 1], -jnp.inf))
                y_heads.append(_dot(xdt.astype(BF16), (cbt * decay).astype(BF16)))
                w_heads.append((xdt * to_edge[k:k + 1, :]).astype(BF16))
            y_d = jnp.concatenate(y_heads, axis=0)
            yt = y_d if yt is None else yt + y_d
            st_ref[d, c] = _dot(jnp.concatenate(w_heads, axis=0), b_bf)
        yt_ref[c] = yt
        return carry

    lax.fori_loop(0, nc, phase_a, 0)

    for d in range(2):
        def step(t, h_in, d=d):
            c = t if d == 0 else nc - 1 - t
            r0 = pl.multiple_of(c * q, q)
            cum_r = cumt_ref[:, pl.ds(r0, q)]
            total = cum_r[:, q - 1:q] if d == 0 else cum_r[:, 0:1]
            own = st_ref[d, c]
            st_ref[d, c] = h_in
            return h_in * head_rows(jnp.broadcast_to(jnp.exp(total), (2 * heads, n)), d) + own

        lax.fori_loop(0, nc, step, jnp.zeros((gl, n), F32))

    def phase_c(c, carry):
        r0 = pl.multiple_of(c * q, q)
        grow = jnp.exp(cumt_ref[:, pl.ds(r0, q)])
        c_bf = cs_ref[pl.ds(r0, q), :]
        yt = yt_ref[c] + xt_ref[c] * dsk_ref[...]
        for d in range(2):
            yt = yt + _dot_nt(st_ref[d, c].astype(BF16), c_bf) * head_rows(grow, d)
        y = jnp.concatenate([yt[:LANES].T, yt[LANES:].T], axis=1)
        y = y * _silu(z_ref[pl.ds(r0, q), :].astype(F32))
        o_ref[pl.ds(r0, q), :] = (_rms(y) * ng_ref[...]).astype(BF16)
        return carry

    lax.fori_loop(0, nc, phase_c, 0)


def _ssd_core_call(z, xbc, cum, dtt, cumt, conv_w_g, conv_b_g, d_skip_col, norm_g, *, batch, seq):
    t, d_inner = z.shape
    n = SSD_STATE
    gl = GROUP_LANES
    x_blocks = d_inner // n
    c_first = x_blocks + SSD_GROUPS
    nc = seq // SSD_CHUNK
    dir_heads = 2 * SSD_HEADS_PER_GROUP
    conv_lanes = conv_w_g.shape[-1]
    in_specs = [
        pl.BlockSpec((seq, gl), lambda b, g: (b, g)),
        pl.BlockSpec((seq, n), lambda b, g: (b, x_blocks + g)),
        pl.BlockSpec((seq, n), lambda b, g: (b, c_first + g)),
        pl.BlockSpec((seq, gl), lambda b, g: (b, g)),
        pl.BlockSpec((seq, LANES), lambda b, g: (b, 0)),
        pl.BlockSpec((None, dir_heads, seq), lambda b, g: (b, g, 0)),
        pl.BlockSpec((None, dir_heads, seq), lambda b, g: (b, g, 0)),
        pl.BlockSpec((None, SSD_CONV, conv_lanes), lambda b, g: (g, 0, 0)),
        pl.BlockSpec((None, 1, conv_lanes), lambda b, g: (g, 0, 0)),
        pl.BlockSpec((gl, LANES), lambda b, g: (g, 0)),
        pl.BlockSpec((1, gl), lambda b, g: (0, g)),
    ]
    return pl.pallas_call(
        _ssd_core_kernel,
        grid=(batch, SSD_GROUPS),
        in_specs=in_specs,
        out_specs=pl.BlockSpec((seq, gl), lambda b, g: (b, g)),
        out_shape=jax.ShapeDtypeStruct((t, d_inner), BF16),
        scratch_shapes=[
            pltpu.VMEM((nc, gl, SSD_CHUNK), F32),
            pltpu.VMEM((seq, n), BF16),
            pltpu.VMEM((nc, gl, SSD_CHUNK), F32),
            pltpu.VMEM((2, nc, gl, n), F32),
        ],
        compiler_params=_params("arbitrary", "arbitrary"),
        name="ssd_core",
    )(xbc, xbc, xbc, z, cum, dtt, cumt, conv_w_g, conv_b_g, d_skip_col, norm_g)


def _mla_in_kernel(x_ref, pos_ref, mod_ref, g_ref, win_ref, qng_ref, kvng_ref, wuq_ref, wk_ref, wv_ref,
                   qhg_ref, khg_ref, freq_ref, sign_ref, vone_ref, q_ref, k_ref, v_ref):
    hl = MLA_HEAD_LANES
    h = _modulated(x_ref[...], g_ref[...], mod_ref, 1).astype(BF16)
    lat = _dot(h, win_ref[...])
    q_lat = (_rms(lat[:, :MLA_Q_LORA]) * qng_ref[...]).astype(BF16)
    kv_lat = (_rms(lat[:, MLA_Q_LORA:MLA_Q_LORA + MLA_KV_LORA]) * kvng_ref[...]).astype(BF16)
    k_pe = lat[:, MLA_Q_LORA + MLA_KV_LORA:]
    qf = _dot(q_lat, wuq_ref[...])
    kf = _dot(kv_lat, wk_ref[...])
    v_ref[...] = (_dot(kv_lat, wv_ref[...]) + vone_ref[...]).astype(BF16)

    ang = pos_ref[...] * freq_ref[...]
    cos = jnp.cos(ang)
    sin = jnp.sin(ang) * sign_ref[...]
    q_scale = MLA_QK ** -0.5 * LOG2_E

    def head_norm_rope(t, gain):
        ms = jnp.sum(t * t, axis=-1, keepdims=True) * (1.0 / MLA_QK)
        t = t * lax.rsqrt(ms + NORM_EPS) * gain
        return t * cos + pltpu.roll(t, hl // 2, 1) * sin

    for hd in range(MLA_HEADS):
        sl = slice(hd * hl, (hd + 1) * hl)
        q_ref[:, sl] = (head_norm_rope(qf[:, sl], qhg_ref[...]) * q_scale).astype(BF16)
        k_ref[:, sl] = head_norm_rope(kf[:, sl] + k_pe, khg_ref[...]).astype(BF16)


def _mla_in_call(x, pos, mod, g, win, qng, kvng, wuq, wk, wv, qhg, khg, freq, sign, vone, *, seq):
    t, d = x.shape
    tm = ROW_TILE
    x_spec, mod_spec = _row_specs(t, d, seq, tm)
    qk_w = MLA_HEADS * MLA_HEAD_LANES
    v_w = qk_w
    return pl.pallas_call(
        _mla_in_kernel,
        grid=(t // tm,),
        in_specs=[x_spec, pl.BlockSpec((tm, 1), lambda i: (i, 0)), mod_spec, _resident((1, d)),
                  _resident(win.shape), _resident(qng.shape), _resident(kvng.shape),
                  _resident(wuq.shape), _resident(wk.shape), _resident(wv.shape),
                  _resident(qhg.shape), _resident(khg.shape), _resident(freq.shape),
                  _resident(sign.shape), _resident(vone.shape)],
        out_specs=[pl.BlockSpec((tm, qk_w), lambda i: (i, 0)),
                   pl.BlockSpec((tm, qk_w), lambda i: (i, 0)),
                   pl.BlockSpec((tm, v_w), lambda i: (i, 0))],
        out_shape=[jax.ShapeDtypeStruct((t, qk_w), BF16),
                   jax.ShapeDtypeStruct((t, qk_w), BF16),
                   jax.ShapeDtypeStruct((t, v_w), BF16)],
        compiler_params=_params("arbitrary"),
        name="mla_in",
    )(x, pos, mod, g, win, qng, kvng, wuq, wk, wv, qhg, khg, freq, sign, vone)


def _attn_kernel(q_ref, k_ref, v_ref, o_ref):
    hl = MLA_HEAD_LANES
    rows = q_ref.shape[0] // ATTN_ROW_SPLIT
    lane = lax.broadcasted_iota(jnp.int32, (rows, o_ref.shape[1]), 1)
    items = [(part, hd) for part in range(ATTN_ROW_SPLIT) for hd in range(2)]
    s = [_dot_nt(q_ref[part * rows:(part + 1) * rows, hd * hl:(hd + 1) * hl],
                 k_ref[:, hd * hl:(hd + 1) * hl]) for part, hd in items]
    outs = []
    for idx, (part, hd) in enumerate(items):
        p = jnp.exp2(s[idx] - jnp.max(s[idx], axis=-1, keepdims=True))
        r = _dot(p.astype(BF16), v_ref[:, hd * hl:(hd + 1) * hl])
        outs.append(r / r[:, MLA_V:MLA_V + 1])
    for part in range(ATTN_ROW_SPLIT):
        second = pltpu.roll(outs[2 * part + 1], MLA_V, 1)
        out = jnp.where(lane >= MLA_V, second, outs[2 * part])
        o_ref[part * rows:(part + 1) * rows, :] = out.astype(BF16)


def _attn_call(q, k, v, *, batch, seq):
    tq = ATTN_Q_TILE
    nq = seq // tq
    pair_qk = 2 * MLA_HEAD_LANES
    pair_v = 2 * MLA_V
    return pl.pallas_call(
        _attn_kernel,
        grid=(batch, MLA_HEADS // 2, nq),
        in_specs=[pl.BlockSpec((tq, pair_qk), lambda b, p, i: (b * nq + i, p)),
                  pl.BlockSpec((seq, pair_qk), lambda b, p, i: (b, p)),
                  pl.BlockSpec((seq, pair_qk), lambda b, p, i: (b, p))],
        out_specs=pl.BlockSpec((tq, pair_v), lambda b, p, i: (b * nq + i, p)),
        out_shape=jax.ShapeDtypeStruct((q.shape[0], MLA_HEADS * MLA_V), BF16),
        compiler_params=_params("arbitrary", "arbitrary", "arbitrary"),
        name="attn",
    )(q, k, v)


def _mla_head_perm():
    half = MLA_ROPE // 2
    perm = [-1] * MLA_HEAD_LANES
    for l in range(half):
        perm[l] = MLA_NOPE + l
        perm[MLA_HEAD_LANES // 2 + l] = MLA_NOPE + half + l
    nope_lo = MLA_HEAD_LANES // 2 - half
    for l in range(nope_lo):
        perm[half + l] = l
    for l in range(MLA_NOPE - nope_lo):
        perm[MLA_HEAD_LANES // 2 + half + l] = nope_lo + l
    return perm


def _gather_cols(w, perm):
    idx = jnp.array([max(p, 0) for p in perm], jnp.int32)
    keep = jnp.array([p >= 0 for p in perm])
    return jnp.where(keep, jnp.take(w, idx, axis=-1), 0.0)


def _mla_layout(w_in, w_uq, w_ukv, q_head_g, k_head_g):
    perm = _mla_head_perm()
    half = MLA_ROPE // 2
    lat = MLA_Q_LORA + MLA_KV_LORA
    pe_perm = [p - MLA_NOPE if p >= MLA_NOPE else -1 for p in perm]
    win = jnp.concatenate([w_in[:, :lat], _gather_cols(w_in[:, lat:], pe_perm)], axis=1)
    wuq = _gather_cols(w_uq.reshape(MLA_Q_LORA, MLA_HEADS, MLA_QK), perm)
    wuq = wuq.reshape(MLA_Q_LORA, MLA_HEADS * MLA_HEAD_LANES)
    kv = w_ukv.reshape(MLA_KV_LORA, MLA_HEADS, MLA_NOPE + MLA_V)
    nope_perm = [p if 0 <= p < MLA_NOPE else -1 for p in perm]
    wk = _gather_cols(kv[:, :, :MLA_NOPE], nope_perm).reshape(MLA_KV_LORA, MLA_HEADS * MLA_HEAD_LANES)
    v_pad = [(0, 0), (0, 0), (0, MLA_HEAD_LANES - MLA_V)]
    wv = jnp.pad(kv[:, :, MLA_NOPE:], v_pad).reshape(MLA_KV_LORA, MLA_HEADS * MLA_HEAD_LANES)
    vone = jnp.tile((jnp.arange(MLA_HEAD_LANES) == MLA_V).astype(F32), MLA_HEADS)[None, :]
    qhg = _gather_cols(q_head_g[None, :], perm)
    khg = _gather_cols(k_head_g[None, :], perm)
    inv_freq = ROPE_THETA ** (-jnp.arange(0, MLA_ROPE, 2, dtype=F32) / MLA_ROPE)
    zeros = jnp.zeros((MLA_HEAD_LANES // 2 - half,), F32)
    freq = jnp.concatenate([inv_freq, zeros, inv_freq, zeros])[None, :]
    sign = jnp.concatenate([-jnp.ones((half,), F32), zeros, jnp.ones((half,), F32), zeros])[None, :]
    return (win.astype(BF16), wuq.astype(BF16), wk.astype(BF16), wv.astype(BF16), qhg, khg, freq, sign,
            vone)


def _ssd_layout(w_in, conv_w, conv_b, dt_bias, a_log, d_skip):
    d_inner = SSD_GROUPS * GROUP_LANES
    bc_ch = SSD_GROUPS * SSD_STATE
    conv_ch = d_inner + 2 * bc_ch
    heads = SSD_GROUPS * SSD_HEADS_PER_GROUP
    wz = w_in[:, :d_inner].astype(BF16)
    wxbc = w_in[:, d_inner:d_inner + conv_ch].astype(BF16)

    def conv_by_group(v):
        rows = v.shape[0]
        parts = [v[:, :d_inner].reshape(rows, SSD_GROUPS, GROUP_LANES),
                 v[:, d_inner:d_inner + bc_ch].reshape(rows, SSD_GROUPS, SSD_STATE),
                 v[:, d_inner + bc_ch:].reshape(rows, SSD_GROUPS, SSD_STATE)]
        return jnp.swapaxes(jnp.concatenate(parts, axis=-1), 0, 1)

    def by_group(v):
        lead = v.shape[:-1]
        v = v.reshape(lead + (2, SSD_GROUPS, SSD_HEADS_PER_GROUP))
        v = jnp.swapaxes(v, -3, -2).reshape(lead + (2 * heads,))
        return jnp.pad(v, [(0, 0)] * len(lead) + [(0, LANES - 2 * heads)])

    wdt = by_group(w_in[:, d_inner + conv_ch:]).astype(BF16)
    dtb = by_group(dt_bias.reshape(1, 2 * heads))
    a_log_cols = by_group(a_log.reshape(1, 2 * heads))
    d_skip_col = jnp.broadcast_to(jnp.repeat(d_skip, SSD_HEAD_DIM)[:, None], (d_inner, LANES))
    return wz, wxbc, wdt, dtb, a_log_cols, d_skip_col, conv_by_group(conv_w), conv_by_group(conv_b[None, :])


def kernel(x, c, positions, norm_g, w_mod, b_mod, ffn_w_gate, ffn_w_up, ffn_w_down, ssd_w_in, ssd_conv_w,
           ssd_conv_b, ssd_dt_bias, ssd_a_log, ssd_d, ssd_norm_g, ssd_w_out, mla_w_in, mla_q_norm_g,
           mla_kv_norm_g, mla_w_uq, mla_w_ukv, mla_q_head_g, mla_k_head_g, mla_w_out):
    batch, seq, d = x.shape
    depth = w_mod.shape[0]
    t = batch * seq
    n_mixers = 2
    xf = x.reshape(t, d)
    pos = positions.astype(F32).reshape(t, 1)
    mod = _mod_call(c, w_mod, b_mod).reshape(depth, batch, 3 * N_SUBLAYERS, d)
    wg, wu, wd = (w.astype(BF16) for w in (ffn_w_gate, ffn_w_up, ffn_w_down))

    for i in range(depth):
        g = norm_g[i][:, None, :]
        xf = _ffn_call(xf, mod[i], g[0], wg[i, 0], wu[i, 0], wd[i, 0], sub=0, seq=seq)
        j = i // n_mixers
        if i % n_mixers == 0:
            wz, wxbc, wdt, dtb, a_log_cols, d_skip_col, conv_w_g, conv_b_g = _ssd_layout(
                ssd_w_in[j], ssd_conv_w[j], ssd_conv_b[j], ssd_dt_bias[j], ssd_a_log[j], ssd_d[j])
            z, xbc, dt_raw = _ssd_in_call(xf, mod[i], g[1], wz, wxbc, wdt, seq=seq)
            cum, dtt, cumt = _ssd_dt_call(dt_raw, dtb, a_log_cols, batch=batch, seq=seq)
            y = _ssd_core_call(z, xbc, cum, dtt, cumt, conv_w_g, conv_b_g, d_skip_col,
                               ssd_norm_g[j][None, :], batch=batch, seq=seq)
            w_out = ssd_w_out[j].astype(BF16)
        else:
            win, wuq, wk, wv, qhg, khg, freq, sign, vone = _mla_layout(
                mla_w_in[j], mla_w_uq[j], mla_w_ukv[j], mla_q_head_g[j], mla_k_head_g[j])
            q, k, v = _mla_in_call(xf, pos, mod[i], g[1], win, mla_q_norm_g[j][None, :],
                                   mla_kv_norm_g[j][None, :], wuq, wk, wv, qhg, khg, freq, sign, vone,
                                   seq=seq)
            y = _attn_call(q, k, v, batch=batch, seq=seq)
            w_out = mla_w_out[j].astype(BF16)
        xf = _proj_ffn_call(xf, y, w_out, mod[i], g[2], wg[i, 1], wu[i, 1], wd[i, 1], seq=seq)
    return xf.reshape(batch, seq, d)
```

```python
import functools

import jax
import jax.numpy as jnp
from jax import lax
from jax.experimental import pallas as pl
from jax.experimental.pallas import tpu as pltpu

F32 = jnp.float32
BF16 = jnp.bfloat16

N_SUBLAYERS = 3
FFN_RES_WEIGHT = 0.5
NORM_EPS = 1e-6
SSD_HEAD_DIM = 64
SSD_GROUPS = 8
SSD_HEADS_PER_GROUP = 4
SSD_STATE = 128
SSD_CONV = 5
SSD_CHUNK = 128
MLA_HEADS = 16
MLA_Q_LORA = 384
MLA_KV_LORA = 256
MLA_NOPE = 64
MLA_ROPE = 32
MLA_V = 64
MLA_QK = MLA_NOPE + MLA_ROPE
ROPE_THETA = 10000.0
LOG2_E = 1.4426950408889634

LANES = 128
BF16_SUBLANES = 16
VMEM_LIMIT_BYTES = 56 * 1024 * 1024

ROW_TILE = 512
ATTN_Q_TILE = 1024
ATTN_ROW_SPLIT = 4
MLA_HEAD_LANES = LANES
GROUP_LANES = SSD_HEADS_PER_GROUP * SSD_HEAD_DIM
SSD_UNROLL = 4


def _params(*semantics):
    return pltpu.CompilerParams(dimension_semantics=semantics,
                                vmem_limit_bytes=VMEM_LIMIT_BYTES)


def _silu(v):
    return v * jax.nn.sigmoid(v)


def _rms(v):
    ms = jnp.mean(v * v, axis=-1, keepdims=True)
    return v * lax.rsqrt(ms + NORM_EPS)


def _modulated(x, g, mod_ref, sub):
    shift = mod_ref[3 * sub:3 * sub + 1, :]
    scale = mod_ref[3 * sub + 1:3 * sub + 2, :]
    return _rms(x) * g * (1.0 + scale) + shift


def _dot(a, b):
    return jnp.dot(a, b, preferred_element_type=F32)


def _dot_nt(a, b):
    return lax.dot_general(a, b, (((1,), (1,)), ((), ())), preferred_element_type=F32)


def _mod_kernel(c_ref, w_ref, b_ref, o_ref):
    o_ref[...] = _dot(_silu(c_ref[...]), w_ref[...]) + b_ref[...]


def _mod_call(c, w_mod, b_mod):
    depth, d, n = w_mod.shape
    bsz = c.shape[0]
    tn = n // 8
    return pl.pallas_call(
        _mod_kernel,
        grid=(depth, n // tn),
        in_specs=[
            pl.BlockSpec((bsz, d), lambda i, j: (0, 0)),
            pl.BlockSpec((None, d, tn), lambda i, j: (i, 0, j)),
            pl.BlockSpec((None, 1, tn), lambda i, j: (i, 0, j)),
        ],
        out_specs=pl.BlockSpec((None, bsz, tn), lambda i, j: (i, 0, j)),
        out_shape=jax.ShapeDtypeStruct((depth, bsz, n), F32),
        compiler_params=_params("arbitrary", "arbitrary"),
        name="mod",
    )(c, w_mod, b_mod.reshape(depth, 1, n))


def _ff_chunks(d_ff):
    half = (d_ff // 512) * 256
    return ((0, half), (half, d_ff)) if 0 < half < d_ff else ((0, d_ff),)


def _ffn_apply(x, mod_ref, g_ref, wg_ref, wu_ref, wd_ref, sub):
    h = _modulated(x, g_ref[...], mod_ref, sub).astype(BF16)
    acc = None
    for lo, hi in _ff_chunks(wg_ref.shape[1]):
        gate = _dot(h, wg_ref[:, lo:hi])
        up = _dot(h, wu_ref[:, lo:hi])
        part = _dot((_silu(gate) * up).astype(BF16), wd_ref[lo:hi, :])
        acc = part if acc is None else acc + part
    res_gate = mod_ref[3 * sub + 2:3 * sub + 3, :]
    return x + (FFN_RES_WEIGHT * res_gate) * acc


def _ffn_kernel(x_ref, mod_ref, g_ref, wg_ref, wu_ref, wd_ref, o_ref, *, sub):
    o_ref[...] = _ffn_apply(x_ref[...], mod_ref, g_ref, wg_ref, wu_ref, wd_ref, sub)


def _proj_ffn_kernel(x_ref, y_ref, wo_ref, mod_ref, g_ref, wg_ref, wu_ref, wd_ref, o_ref):
    mix_gate = mod_ref[5:6, :]
    x = x_ref[...] + mix_gate * _dot(y_ref[...], wo_ref[...])
    o_ref[...] = _ffn_apply(x, mod_ref, g_ref, wg_ref, wu_ref, wd_ref, 2)


def _resident(shape):
    return pl.BlockSpec(shape, lambda *_: (0,) * len(shape), pipeline_mode=pl.Buffered(1))


def _row_specs(t, d, seq, tm):
    tiles_per_seq = seq // tm
    x_spec = pl.BlockSpec((tm, d), lambda i: (i, 0))
    mod_spec = pl.BlockSpec((None, 3 * N_SUBLAYERS, d), lambda i: (i // tiles_per_seq, 0, 0))
    return x_spec, mod_spec


def _ffn_call(x, mod, g, wg, wu, wd, *, sub, seq):
    t, d = x.shape
    tm = ROW_TILE
    x_spec, mod_spec = _row_specs(t, d, seq, tm)
    return pl.pallas_call(
        functools.partial(_ffn_kernel, sub=sub),
        grid=(t // tm,),
        in_specs=[x_spec, mod_spec, _resident((1, d)), _resident(wg.shape), _resident(wu.shape),
                  _resident(wd.shape)],
        out_specs=x_spec,
        out_shape=jax.ShapeDtypeStruct((t, d), F32),
        compiler_params=_params("arbitrary"),
        name="ffn",
    )(x, mod, g, wg, wu, wd)


def _proj_ffn_call(x, y, wo, mod, g, wg, wu, wd, *, seq):
    t, d = x.shape
    tm = ROW_TILE
    x_spec, mod_spec = _row_specs(t, d, seq, tm)
    y_spec = pl.BlockSpec((tm, y.shape[1]), lambda i: (i, 0))
    return pl.pallas_call(
        _proj_ffn_kernel,
        grid=(t // tm,),
        in_specs=[x_spec, y_spec, _resident(wo.shape), mod_spec, _resident((1, d)), _resident(wg.shape),
                  _resident(wu.shape), _resident(wd.shape)],
        out_specs=x_spec,
        out_shape=jax.ShapeDtypeStruct((t, d), F32),
        compiler_params=_params("arbitrary"),
        name="proj_ffn",
    )(x, y, wo, mod, g, wg, wu, wd)


def _ssd_in_kernel(x_ref, mod_ref, g_ref, wz_ref, wxbc_ref, wdt_ref, z_ref, xbc_ref, dt_ref):
    h = _modulated(x_ref[...], g_ref[...], mod_ref, 1).astype(BF16)
    z_ref[...] = _dot(h, wz_ref[...]).astype(BF16)
    xbc_ref[...] = _dot(h, wxbc_ref[...]).astype(BF16)
    dt_ref[...] = _dot(h, wdt_ref[...])


def _ssd_in_call(x, mod, g, wz, wxbc, wdt, *, seq):
    t, d = x.shape
    tm = ROW_TILE
    x_spec, mod_spec = _row_specs(t, d, seq, tm)
    widths = (wz.shape[1], wxbc.shape[1], wdt.shape[1])
    return pl.pallas_call(
        _ssd_in_kernel,
        grid=(t // tm,),
        in_specs=[x_spec, mod_spec, _resident((1, d)), _resident(wz.shape), _resident(wxbc.shape),
                  _resident(wdt.shape)],
        out_specs=[pl.BlockSpec((tm, w), lambda i: (i, 0)) for w in widths],
        out_shape=[jax.ShapeDtypeStruct((t, widths[0]), BF16),
                   jax.ShapeDtypeStruct((t, widths[1]), BF16),
                   jax.ShapeDtypeStruct((t, widths[2]), F32)],
        compiler_params=_params("arbitrary"),
        name="ssd_in",
    )(x, mod, g, wz, wxbc, wdt)


def _split3(v):
    v1 = v.astype(BF16)
    rem = v - v1.astype(F32)
    v2 = rem.astype(BF16)
    return v1, v2, (rem - v2.astype(F32)).astype(BF16)


def _ssd_dt_kernel(dtr_ref, dtb_ref, alog_ref, cum_ref, dtt_ref, cumt_ref):
    seq = dtr_ref.shape[0]
    q = SSD_CHUNK
    row_i = lax.broadcasted_iota(jnp.int32, (q, q), 0)
    col_j = lax.broadcasted_iota(jnp.int32, (q, q), 1)
    tri = jnp.where(row_i >= col_j, 1.0, 0.0).astype(BF16)
    lane = lax.broadcasted_iota(jnp.int32, (q, LANES), 1)
    is_fwd = (lane // SSD_HEADS_PER_GROUP) % 2 == 0
    a_coef = -jnp.exp(alog_ref[...]) * LOG2_E

    def chunk(c, carry):
        r0 = pl.multiple_of(c * q, q)
        dt = jax.nn.softplus(dtr_ref[pl.ds(r0, q), :] + dtb_ref[...])
        a = dt * a_coef
        a1, a2, a3 = _split3(a)
        prefix = _dot(tri, a1) + _dot(tri, a2) + _dot(tri, a3)
        cum = jnp.where(is_fwd, prefix, prefix[q - 1:q, :] - prefix + a)
        cum_ref[pl.ds(r0, q), :] = cum
        dtt_ref[:, pl.ds(r0, q)] = dt.T
        cumt_ref[:, pl.ds(r0, q)] = cum.T
        return carry

    lax.fori_loop(0, seq // q, chunk, 0)


def _ssd_dt_call(dt_raw, dt_bias, a_log, *, batch, seq):
    t = dt_raw.shape[0]
    head_major = pl.BlockSpec((None, LANES, seq), lambda b: (b, 0, 0))
    return pl.pallas_call(
        _ssd_dt_kernel,
        grid=(batch,),
        in_specs=[pl.BlockSpec((seq, LANES), lambda b: (b, 0)),
                  pl.BlockSpec((1, LANES), lambda b: (0, 0)),
                  pl.BlockSpec((1, LANES), lambda b: (0, 0))],
        out_specs=[pl.BlockSpec((seq, LANES), lambda b: (b, 0)), head_major, head_major],
        out_shape=[jax.ShapeDtypeStruct((t, LANES), F32),
                   jax.ShapeDtypeStruct((batch, LANES, seq), F32),
                   jax.ShapeDtypeStruct((batch, LANES, seq), F32)],
        compiler_params=_params("arbitrary"),
        name="ssd_dt",
    )(dt_raw, dt_bias, a_log)


def _ssd_core_kernel(xr_ref, br_ref, cr_ref, z_ref, cum_ref, dtt_ref, cumt_ref, cw_ref, cb_ref,
                     dsk_ref, ng_ref, o_ref, xt_ref, cs_ref, yt_ref, st_ref, dec_ref):
    seq = xr_ref.shape[0]
    q = SSD_CHUNK
    halo = BF16_SUBLANES
    nc = seq // q
    gl = GROUP_LANES
    n = SSD_STATE
    hp = SSD_HEAD_DIM
    heads = SSD_HEADS_PER_GROUP
    centre = SSD_CONV // 2
    group = pl.program_id(1)

    col_shift = jnp.where(group == 0, 0, LANES - 2 * heads * group)
    src_j = lax.broadcasted_iota(jnp.int32, (q, q), 0)
    dst_i = lax.broadcasted_iota(jnp.int32, (q, q), 1)
    causal = (dst_i >= src_j, dst_i <= src_j)
    taps = [k for k in range(SSD_CONV) if k != centre]
    tap_row = lax.broadcasted_iota(jnp.int32, (q, q + 2 * halo), 0)
    tap_col = lax.broadcasted_iota(jnp.int32, (q, q + 2 * halo), 1)
    shift_all = jnp.concatenate(
        [jnp.where(tap_col == tap_row + (halo + k - centre), 1.0, 0.0).astype(BF16) for k in taps], axis=0)

    def rows_of(ref, r0, r1, r2):
        return jnp.concatenate([ref[pl.ds(r0, halo), :], ref[pl.ds(r1, q), :], ref[pl.ds(r2, halo), :]],
                               axis=0)

    def head_rows(vals, direction):
        return jnp.concatenate(
            [jnp.broadcast_to(vals[heads * direction + h:heads * direction + h + 1, :], (hp, vals.shape[1]))
             for h in range(heads)], axis=0)

    def conv_part(c):
        r0 = pl.multiple_of(c * q, q)
        rp = pl.multiple_of(jnp.maximum(r0 - halo, 0), halo)
        rn = pl.multiple_of(jnp.minimum(r0 + q, seq - halo), halo)
        win = jnp.concatenate([rows_of(xr_ref, rp, r0, rn), rows_of(br_ref, rp, r0, rn),
                               rows_of(cr_ref, rp, r0, rn)], axis=1)
        row = lax.broadcasted_iota(jnp.int32, (q + 2 * halo, 1), 0)
        outside = ((row < halo) & (c == 0)) | ((row >= halo + q) & (c == nc - 1))
        win = jnp.where(outside, jnp.zeros_like(win), win)
        acc = cb_ref[...] + cw_ref[centre:centre + 1, :] * win[halo:halo + q].astype(F32)
        shifted = _dot(shift_all, win)
        for idx, k in enumerate(taps):
            acc = acc + cw_ref[k:k + 1, :] * shifted[idx * q:(idx + 1) * q]
        act = _silu(acc)
        x = act[:, :gl]
        xt = jnp.concatenate([x[:, :LANES].T, x[:, LANES:].T], axis=0)
        b_bf = act[:, gl:gl + n].astype(BF16)
        c_bf = act[:, gl + n:].astype(BF16)
        xt_ref[c] = xt
        cs_ref[pl.ds(r0, q), :] = c_bf
        return xt, b_bf, c_bf

    def local_part(c, xt, b_bf, cbt):
        r0 = pl.multiple_of(c * q, q)
        cum_c = pltpu.roll(cum_ref[pl.ds(r0, q), :], col_shift, 1)
        dt_r = dtt_ref[:, pl.ds(r0, q)]
        cum_r = cumt_ref[:, pl.ds(r0, q)]
        total = jnp.concatenate([cum_r[:heads, q - 1:q], cum_r[heads:, 0:1]], axis=0)
        to_edge = jnp.exp2(total - cum_r)
        dec_ref[c] = jnp.broadcast_to(jnp.exp2(total), (2 * heads, n))
        xdt = [xt[(k % heads) * hp:(k % heads + 1) * hp, :] * dt_r[k:k + 1, :] for k in range(2 * heads)]
        st_ref[c] = _dot(jnp.concatenate([(xdt[k] * to_edge[k:k + 1, :]).astype(BF16)
                                          for k in range(2 * heads)], axis=0), b_bf)
        y_heads = []
        for h in range(heads):
            ks = (h, heads + h)
            lhs = jnp.concatenate([xdt[k].astype(BF16) for k in ks], axis=1)
            rhs = jnp.concatenate(
                [(cbt * jnp.exp2(jnp.where(causal[k // heads], cum_r[k:k + 1, :] - cum_c[:, k:k + 1],
                                           -jnp.inf))).astype(BF16) for k in ks], axis=0)
            y_heads.append(_dot(lhs, rhs))
        yt_ref[c] = jnp.concatenate(y_heads, axis=0)

    def phase_a(c, conv_c):
        xt, b_bf, c_bf = conv_c
        cbt = _dot_nt(b_bf, c_bf)
        conv_next = conv_part(c + 1)
        local_part(c, xt, b_bf, cbt)
        return conv_next

    xt, b_bf, c_bf = lax.fori_loop(0, nc - 1, phase_a, conv_part(0), unroll=SSD_UNROLL)
    local_part(nc - 1, xt, b_bf, _dot_nt(b_bf, c_bf))

    def step(t, carry):
        new = []
        for d in range(2):
            c = t if d == 0 else nc - 1 - t
            own = st_ref[c, d * gl:(d + 1) * gl]
            st_ref[c, d * gl:(d + 1) * gl] = carry[d]
            new.append(carry[d] * head_rows(dec_ref[c], d) + own)
        return tuple(new)

    lax.fori_loop(0, nc, step, (jnp.zeros((gl, n), F32), jnp.zeros((gl, n), F32)))

    def phase_c(c, carry):
        r0 = pl.multiple_of(c * q, q)
        grow = jnp.exp2(cumt_ref[:, pl.ds(r0, q)])
        c_bf = cs_ref[pl.ds(r0, q), :]
        off = _dot_nt(st_ref[c].astype(BF16), c_bf)
        yt = yt_ref[c] + xt_ref[c] * dsk_ref[...]
        for d in range(2):
            yt = yt + off[d * gl:(d + 1) * gl] * head_rows(grow, d)
        y = jnp.concatenate([yt[:LANES].T, yt[LANES:].T], axis=1)
        y = y * _silu(z_ref[pl.ds(r0, q), :].astype(F32))
        o_ref[pl.ds(r0, q), :] = (_rms(y) * ng_ref[...]).astype(BF16)
        return carry

    lax.fori_loop(0, nc, phase_c, 0, unroll=SSD_UNROLL)


def _ssd_core_call(z, xbc, cum, dtt, cumt, conv_w_g, conv_b_g, d_skip_col, norm_g, *, batch, seq):
    t, d_inner = z.shape
    n = SSD_STATE
    gl = GROUP_LANES
    x_blocks = d_inner // n
    c_first = x_blocks + SSD_GROUPS
    nc = seq // SSD_CHUNK
    dir_heads = 2 * SSD_HEADS_PER_GROUP
    conv_lanes = conv_w_g.shape[-1]
    in_specs = [
        pl.BlockSpec((seq, gl), lambda b, g: (b, g)),
        pl.BlockSpec((seq, n), lambda b, g: (b, x_blocks + g)),
        pl.BlockSpec((seq, n), lambda b, g: (b, c_first + g)),
        pl.BlockSpec((seq, gl), lambda b, g: (b, g)),
        pl.BlockSpec((seq, LANES), lambda b, g: (b, 0)),
        pl.BlockSpec((None, dir_heads, seq), lambda b, g: (b, g, 0)),
        pl.BlockSpec((None, dir_heads, seq), lambda b, g: (b, g, 0)),
        pl.BlockSpec((None, SSD_CONV, conv_lanes), lambda b, g: (g, 0, 0)),
        pl.BlockSpec((None, 1, conv_lanes), lambda b, g: (g, 0, 0)),
        pl.BlockSpec((gl, LANES), lambda b, g: (g, 0)),
        pl.BlockSpec((1, gl), lambda b, g: (0, g)),
    ]
    return pl.pallas_call(
        _ssd_core_kernel,
        grid=(batch, SSD_GROUPS),
        in_specs=in_specs,
        out_specs=pl.BlockSpec((seq, gl), lambda b, g: (b, g)),
        out_shape=jax.ShapeDtypeStruct((t, d_inner), BF16),
        scratch_shapes=[
            pltpu.VMEM((nc, gl, SSD_CHUNK), F32),
            pltpu.VMEM((seq, n), BF16),
            pltpu.VMEM((nc, gl, SSD_CHUNK), F32),
            pltpu.VMEM((nc, 2 * gl, n), F32),
            pltpu.VMEM((nc, dir_heads, n), F32),
        ],
        compiler_params=_params("arbitrary", "arbitrary"),
        name="ssd_core",
    )(xbc, xbc, xbc, z, cum, dtt, cumt, conv_w_g, conv_b_g, d_skip_col, norm_g)


def _mla_in_kernel(x_ref, pos_ref, mod_ref, g_ref, win_ref, qng_ref, kvng_ref, wuq_ref, wk_ref, wv_ref,
                   qhg_ref, khg_ref, freq_ref, vone_ref, q_ref, k_ref, v_ref):
    hl = MLA_HEAD_LANES
    tm = x_ref.shape[0]
    half = MLA_ROPE // 2
    h = _modulated(x_ref[...], g_ref[...], mod_ref, 1).astype(BF16)
    lat = _dot(h, win_ref[...])
    q_lat = (_rms(lat[:, :MLA_Q_LORA]) * qng_ref[...]).astype(BF16)
    kv_lat = (_rms(lat[:, MLA_Q_LORA:MLA_Q_LORA + MLA_KV_LORA]) * kvng_ref[...]).astype(BF16)
    k_pe = lat[:, MLA_Q_LORA + MLA_KV_LORA:]
    qf = _dot(q_lat, wuq_ref[...])
    kf = _dot(kv_lat, wk_ref[...])
    v_ref[...] = (_dot(kv_lat, wv_ref[...]) + vone_ref[...]).astype(BF16)

    ang = freq_ref[...] * pos_ref[...]
    cos_f, sin_f = jnp.cos(ang), jnp.sin(ang)
    rest = hl // 2 - half
    cos_t = jnp.concatenate([cos_f, jnp.ones((rest, tm), F32)] * 2, axis=0)
    sin_t = jnp.concatenate([-sin_f, jnp.zeros((rest, tm), F32), sin_f, jnp.zeros((rest, tm), F32)], axis=0)

    def token_major(tab):
        return jnp.concatenate([tab[:, r:r + LANES].T for r in range(0, tm, LANES)], axis=0)

    cos = jnp.concatenate([token_major(cos_t)] * 2, axis=1)
    sin = jnp.concatenate([token_major(sin_t)] * 2, axis=1)
    q_scale = MLA_QK ** -0.5 * LOG2_E
    blk_r = lax.broadcasted_iota(jnp.int32, (2 * hl, 2 * hl), 0) // hl
    blk_c = lax.broadcasted_iota(jnp.int32, (2 * hl, 2 * hl), 1) // hl
    head_ones = jnp.where(blk_r == blk_c, 1.0, 0.0).astype(BF16)

    def pair_norm_rope(t, gain):
        ms = _dot((t * t).astype(BF16), head_ones) * (1.0 / MLA_QK)
        t = t * lax.rsqrt(ms + NORM_EPS) * gain
        partner = jnp.concatenate([pltpu.roll(t[:, :hl], hl // 2, 1), pltpu.roll(t[:, hl:], hl // 2, 1)], axis=1)
        return t * cos + partner * sin

    q_gain = jnp.concatenate([qhg_ref[...]] * 2, axis=1)
    k_gain = jnp.concatenate([khg_ref[...]] * 2, axis=1)
    k_pe2 = jnp.concatenate([k_pe] * 2, axis=1)
    for pr in range(MLA_HEADS // 2):
        sl = slice(2 * pr * hl, 2 * (pr + 1) * hl)
        q_ref[:, sl] = (pair_norm_rope(qf[:, sl], q_gain) * q_scale).astype(BF16)
        k_ref[:, sl] = pair_norm_rope(kf[:, sl] + k_pe2, k_gain).astype(BF16)


def _mla_in_call(x, pos, mod, g, win, qng, kvng, wuq, wk, wv, qhg, khg, freq, vone, *, seq):
    t, d = x.shape
    tm = ROW_TILE
    x_spec, mod_spec = _row_specs(t, d, seq, tm)
    qk_w = MLA_HEADS * MLA_HEAD_LANES
    v_w = qk_w
    return pl.pallas_call(
        _mla_in_kernel,
        grid=(t // tm,),
        in_specs=[x_spec, pl.BlockSpec((None, 1, tm), lambda i: (i, 0, 0)), mod_spec, _resident((1, d)),
                  _resident(win.shape), _resident(qng.shape), _resident(kvng.shape),
                  _resident(wuq.shape), _resident(wk.shape), _resident(wv.shape),
                  _resident(qhg.shape), _resident(khg.shape), _resident(freq.shape),
                  _resident(vone.shape)],
        out_specs=[pl.BlockSpec((tm, qk_w), lambda i: (i, 0)),
                   pl.BlockSpec((tm, qk_w), lambda i: (i, 0)),
                   pl.BlockSpec((tm, v_w), lambda i: (i, 0))],
        out_shape=[jax.ShapeDtypeStruct((t, qk_w), BF16),
                   jax.ShapeDtypeStruct((t, qk_w), BF16),
                   jax.ShapeDtypeStruct((t, v_w), BF16)],
        compiler_params=_params("arbitrary"),
        name="mla_in",
    )(x, pos.reshape(t // tm, 1, tm), mod, g, win, qng, kvng, wuq, wk, wv, qhg, khg, freq, vone)


def _attn_kernel(q_ref, k_ref, v_ref, o_ref):
    hl = MLA_HEAD_LANES
    rows = q_ref.shape[0] // ATTN_ROW_SPLIT
    lane = lax.broadcasted_iota(jnp.int32, (rows, o_ref.shape[1]), 1)
    items = [(part, hd) for part in range(ATTN_ROW_SPLIT) for hd in range(2)]
    s = [_dot_nt(q_ref[part * rows:(part + 1) * rows, hd * hl:(hd + 1) * hl],
                 k_ref[:, hd * hl:(hd + 1) * hl]) for part, hd in items]
    outs = []
    for idx, (part, hd) in enumerate(items):
        p = jnp.exp2(s[idx] - jnp.max(s[idx], axis=-1, keepdims=True))
        r = _dot(p.astype(BF16), v_ref[:, hd * hl:(hd + 1) * hl])
        outs.append(r / r[:, MLA_V:MLA_V + 1])
    for part in range(ATTN_ROW_SPLIT):
        second = pltpu.roll(outs[2 * part + 1], MLA_V, 1)
        out = jnp.where(lane >= MLA_V, second, outs[2 * part])
        o_ref[part * rows:(part + 1) * rows, :] = out.astype(BF16)


def _attn_call(q, k, v, *, batch, seq):
    tq = ATTN_Q_TILE
    nq = seq // tq
    pair_qk = 2 * MLA_HEAD_LANES
    pair_v = 2 * MLA_V
    return pl.pallas_call(
        _attn_kernel,
        grid=(batch, MLA_HEADS // 2, nq),
        in_specs=[pl.BlockSpec((tq, pair_qk), lambda b, p, i: (b * nq + i, p)),
                  pl.BlockSpec((seq, pair_qk), lambda b, p, i: (b, p)),
                  pl.BlockSpec((seq, pair_qk), lambda b, p, i: (b, p))],
        out_specs=pl.BlockSpec((tq, pair_v), lambda b, p, i: (b * nq + i, p)),
        out_shape=jax.ShapeDtypeStruct((q.shape[0], MLA_HEADS * MLA_V), BF16),
        compiler_params=_params("arbitrary", "arbitrary", "arbitrary"),
        name="attn",
    )(q, k, v)


def _mla_head_perm():
    half = MLA_ROPE // 2
    perm = [-1] * MLA_HEAD_LANES
    for l in range(half):
        perm[l] = MLA_NOPE + l
        perm[MLA_HEAD_LANES // 2 + l] = MLA_NOPE + half + l
    nope_lo = MLA_HEAD_LANES // 2 - half
    for l in range(nope_lo):
        perm[half + l] = l
    for l in range(MLA_NOPE - nope_lo):
        perm[MLA_HEAD_LANES // 2 + half + l] = nope_lo + l
    return perm


def _gather_cols(w, perm):
    idx = jnp.array([max(p, 0) for p in perm], jnp.int32)
    keep = jnp.array([p >= 0 for p in perm])
    return jnp.where(keep, jnp.take(w, idx, axis=-1), 0.0)


def _mla_layout(w_in, w_uq, w_ukv, q_head_g, k_head_g):
    perm = _mla_head_perm()
    half = MLA_ROPE // 2
    lat = MLA_Q_LORA + MLA_KV_LORA
    pe_perm = [p - MLA_NOPE if p >= MLA_NOPE else -1 for p in perm]
    win = jnp.concatenate([w_in[:, :lat], _gather_cols(w_in[:, lat:], pe_perm)], axis=1)
    wuq = _gather_cols(w_uq.reshape(MLA_Q_LORA, MLA_HEADS, MLA_QK), perm)
    wuq = wuq.reshape(MLA_Q_LORA, MLA_HEADS * MLA_HEAD_LANES)
    kv = w_ukv.reshape(MLA_KV_LORA, MLA_HEADS, MLA_NOPE + MLA_V)
    nope_perm = [p if 0 <= p < MLA_NOPE else -1 for p in perm]
    wk = _gather_cols(kv[:, :, :MLA_NOPE], nope_perm).reshape(MLA_KV_LORA, MLA_HEADS * MLA_HEAD_LANES)
    v_pad = [(0, 0), (0, 0), (0, MLA_HEAD_LANES - MLA_V)]
    wv = jnp.pad(kv[:, :, MLA_NOPE:], v_pad).reshape(MLA_KV_LORA, MLA_HEADS * MLA_HEAD_LANES)
    vone = jnp.tile((jnp.arange(MLA_HEAD_LANES) == MLA_V).astype(F32), MLA_HEADS)[None, :]
    qhg = _gather_cols(q_head_g[None, :], perm)
    khg = _gather_cols(k_head_g[None, :], perm)
    freq = (ROPE_THETA ** (-jnp.arange(0, MLA_ROPE, 2, dtype=F32) / MLA_ROPE))[:, None]
    return (win.astype(BF16), wuq.astype(BF16), wk.astype(BF16), wv.astype(BF16), qhg, khg, freq, vone)


def _ssd_layout(w_in, conv_w, conv_b, dt_bias, a_log, d_skip):
    d_inner = SSD_GROUPS * GROUP_LANES
    bc_ch = SSD_GROUPS * SSD_STATE
    conv_ch = d_inner + 2 * bc_ch
    heads = SSD_GROUPS * SSD_HEADS_PER_GROUP
    wz = w_in[:, :d_inner].astype(BF16)
    wxbc = w_in[:, d_inner:d_inner + conv_ch].astype(BF16)

    def conv_by_group(v):
        rows = v.shape[0]
        parts = [v[:, :d_inner].reshape(rows, SSD_GROUPS, GROUP_LANES),
                 v[:, d_inner:d_inner + bc_ch].reshape(rows, SSD_GROUPS, SSD_STATE),
                 v[:, d_inner + bc_ch:].reshape(rows, SSD_GROUPS, SSD_STATE)]
        return jnp.swapaxes(jnp.concatenate(parts, axis=-1), 0, 1)

    def by_group(v):
        lead = v.shape[:-1]
        v = v.reshape(lead + (2, SSD_GROUPS, SSD_HEADS_PER_GROUP))
        v = jnp.swapaxes(v, -3, -2).reshape(lead + (2 * heads,))
        return jnp.pad(v, [(0, 0)] * len(lead) + [(0, LANES - 2 * heads)])

    wdt = by_group(w_in[:, d_inner + conv_ch:]).astype(BF16)
    dtb = by_group(dt_bias.reshape(1, 2 * heads))
    a_log_cols = by_group(a_log.reshape(1, 2 * heads))
    d_skip_col = jnp.broadcast_to(jnp.repeat(d_skip, SSD_HEAD_DIM)[:, None], (d_inner, LANES))
    return wz, wxbc, wdt, dtb, a_log_cols, d_skip_col, conv_by_group(conv_w), conv_by_group(conv_b[None, :])


def kernel(x, c, positions, norm_g, w_mod, b_mod, ffn_w_gate, ffn_w_up, ffn_w_down, ssd_w_in, ssd_conv_w,
           ssd_conv_b, ssd_dt_bias, ssd_a_log, ssd_d, ssd_norm_g, ssd_w_out, mla_w_in, mla_q_norm_g,
           mla_kv_norm_g, mla_w_uq, mla_w_ukv, mla_q_head_g, mla_k_head_g, mla_w_out):
    batch, seq, d = x.shape
    depth = w_mod.shape[0]
    t = batch * seq
    n_mixers = 2
    xf = x.reshape(t, d)
    pos = positions.astype(F32).reshape(t, 1)
    mod = _mod_call(c, w_mod, b_mod).reshape(depth, batch, 3 * N_SUBLAYERS, d)
    wg, wu, wd = (w.astype(BF16) for w in (ffn_w_gate, ffn_w_up, ffn_w_down))

    for i in range(depth):
        g = norm_g[i][:, None, :]
        xf = _ffn_call(xf, mod[i], g[0], wg[i, 0], wu[i, 0], wd[i, 0], sub=0, seq=seq)
        j = i // n_mixers
        if i % n_mixers == 0:
            wz, wxbc, wdt, dtb, a_log_cols, d_skip_col, conv_w_g, conv_b_g = _ssd_layout(
                ssd_w_in[j], ssd_conv_w[j], ssd_conv_b[j], ssd_dt_bias[j], ssd_a_log[j], ssd_d[j])
            z, xbc, dt_raw = _ssd_in_call(xf, mod[i], g[1], wz, wxbc, wdt, seq=seq)
            cum, dtt, cumt = _ssd_dt_call(dt_raw, dtb, a_log_cols, batch=batch, seq=seq)
            y = _ssd_core_call(z, xbc, cum, dtt, cumt, conv_w_g, conv_b_g, d_skip_col,
                               ssd_norm_g[j][None, :], batch=batch, seq=seq)
            w_out = ssd_w_out[j].astype(BF16)
        else:
            win, wuq, wk, wv, qhg, khg, freq, vone = _mla_layout(
                mla_w_in[j], mla_w_uq[j], mla_w_ukv[j], mla_q_head_g[j], mla_k_head_g[j])
            q, k, v = _mla_in_call(xf, pos, mod[i], g[1], win, mla_q_norm_g[j][None, :],
                                   mla_kv_norm_g[j][None, :], wuq, wk, wv, qhg, khg, freq, vone, seq=seq)
            y = _attn_call(q, k, v, batch=batch, seq=seq)
            w_out = mla_w_out[j].astype(BF16)
        xf = _proj_ffn_call(xf, y, w_out, mod[i], g[2], wg[i, 1], wu[i, 1], wd[i, 1], seq=seq)
    return xf.reshape(batch, seq, d)
```

```python
import functools

import jax
import jax.numpy as jnp
from jax import lax
from jax.experimental import pallas as pl
from jax.experimental.pallas import tpu as pltpu

F32 = jnp.float32
BF16 = jnp.bfloat16

N_SUBLAYERS = 3
FFN_RES_WEIGHT = 0.5
NORM_EPS = 1e-6
SSD_HEAD_DIM = 64
SSD_GROUPS = 8
SSD_HEADS_PER_GROUP = 4
SSD_STATE = 128
SSD_CONV = 5
SSD_CHUNK = 128
MLA_HEADS = 16
MLA_Q_LORA = 384
MLA_KV_LORA = 256
MLA_NOPE = 64
MLA_ROPE = 32
MLA_V = 64
MLA_QK = MLA_NOPE + MLA_ROPE
ROPE_THETA = 10000.0
LOG2_E = 1.4426950408889634

LANES = 128
BF16_SUBLANES = 16
VMEM_LIMIT_BYTES = 56 * 1024 * 1024

ROW_TILE = 512
ATTN_Q_TILE = 1024
ATTN_ROW_SPLIT = 4
MLA_HEAD_LANES = LANES
GROUP_LANES = SSD_HEADS_PER_GROUP * SSD_HEAD_DIM
SSD_UNROLL = 4


def _params(*semantics):
    return pltpu.CompilerParams(dimension_semantics=semantics,
                                vmem_limit_bytes=VMEM_LIMIT_BYTES)


def _silu(v):
    return v * jax.nn.sigmoid(v)


def _rms(v):
    ms = jnp.mean(v * v, axis=-1, keepdims=True)
    return v * lax.rsqrt(ms + NORM_EPS)


def _modulated(x, g, mod_ref, sub):
    shift = mod_ref[3 * sub:3 * sub + 1, :]
    scale = mod_ref[3 * sub + 1:3 * sub + 2, :]
    return _rms(x) * g * (1.0 + scale) + shift


def _dot(a, b):
    return jnp.dot(a, b, preferred_element_type=F32)


def _dot_nt(a, b):
    return lax.dot_general(a, b, (((1,), (1,)), ((), ())), preferred_element_type=F32)


def _mod_kernel(c_ref, w_ref, b_ref, o_ref):
    o_ref[...] = _dot(_silu(c_ref[...]), w_ref[...]) + b_ref[...]


def _mod_call(c, w_mod, b_mod):
    depth, d, n = w_mod.shape
    bsz = c.shape[0]
    tn = n // 8
    return pl.pallas_call(
        _mod_kernel,
        grid=(depth, n // tn),
        in_specs=[
            pl.BlockSpec((bsz, d), lambda i, j: (0, 0)),
            pl.BlockSpec((None, d, tn), lambda i, j: (i, 0, j)),
            pl.BlockSpec((None, 1, tn), lambda i, j: (i, 0, j)),
        ],
        out_specs=pl.BlockSpec((None, bsz, tn), lambda i, j: (i, 0, j)),
        out_shape=jax.ShapeDtypeStruct((depth, bsz, n), F32),
        compiler_params=_params("arbitrary", "arbitrary"),
        name="mod",
    )(c, w_mod, b_mod.reshape(depth, 1, n))


def _ff_chunks(d_ff):
    half = (d_ff // 512) * 256
    return ((0, half), (half, d_ff)) if 0 < half < d_ff else ((0, d_ff),)


def _ffn_apply(x, mod_ref, g_ref, wg_ref, wu_ref, wd_ref, sub):
    h = _modulated(x, g_ref[...], mod_ref, sub).astype(BF16)
    acc = None
    for lo, hi in _ff_chunks(wg_ref.shape[1]):
        gate = _dot(h, wg_ref[:, lo:hi])
        up = _dot(h, wu_ref[:, lo:hi])
        part = _dot((_silu(gate) * up).astype(BF16), wd_ref[lo:hi, :])
        acc = part if acc is None else acc + part
    res_gate = mod_ref[3 * sub + 2:3 * sub + 3, :]
    return x + (FFN_RES_WEIGHT * res_gate) * acc


def _ffn_kernel(x_ref, mod_ref, g_ref, wg_ref, wu_ref, wd_ref, o_ref, *, sub):
    o_ref[...] = _ffn_apply(x_ref[...], mod_ref, g_ref, wg_ref, wu_ref, wd_ref, sub)


def _proj_ffn_kernel(x_ref, y_ref, wo_ref, mod_ref, g_ref, wg_ref, wu_ref, wd_ref, o_ref):
    mix_gate = mod_ref[5:6, :]
    x = x_ref[...] + mix_gate * _dot(y_ref[...], wo_ref[...])
    o_ref[...] = _ffn_apply(x, mod_ref, g_ref, wg_ref, wu_ref, wd_ref, 2)


def _resident(shape):
    return pl.BlockSpec(shape, lambda *_: (0,) * len(shape), pipeline_mode=pl.Buffered(1))


def _row_specs(t, d, seq, tm):
    tiles_per_seq = seq // tm
    x_spec = pl.BlockSpec((tm, d), lambda i: (i, 0))
    mod_spec = pl.BlockSpec((None, 3 * N_SUBLAYERS, d), lambda i: (i // tiles_per_seq, 0, 0))
    return x_spec, mod_spec


def _ffn_call(x, mod, g, wg, wu, wd, *, sub, seq):
    t, d = x.shape
    tm = ROW_TILE
    x_spec, mod_spec = _row_specs(t, d, seq, tm)
    return pl.pallas_call(
        functools.partial(_ffn_kernel, sub=sub),
        grid=(t // tm,),
        in_specs=[x_spec, mod_spec, _resident((1, d)), _resident(wg.shape), _resident(wu.shape),
                  _resident(wd.shape)],
        out_specs=x_spec,
        out_shape=jax.ShapeDtypeStruct((t, d), F32),
        compiler_params=_params("arbitrary"),
        name="ffn",
    )(x, mod, g, wg, wu, wd)


def _proj_ffn_call(x, y, wo, mod, g, wg, wu, wd, *, seq):
    t, d = x.shape
    tm = ROW_TILE
    x_spec, mod_spec = _row_specs(t, d, seq, tm)
    y_spec = pl.BlockSpec((tm, y.shape[1]), lambda i: (i, 0))
    return pl.pallas_call(
        _proj_ffn_kernel,
        grid=(t // tm,),
        in_specs=[x_spec, y_spec, _resident(wo.shape), mod_spec, _resident((1, d)), _resident(wg.shape),
                  _resident(wu.shape), _resident(wd.shape)],
        out_specs=x_spec,
        out_shape=jax.ShapeDtypeStruct((t, d), F32),
        compiler_params=_params("arbitrary"),
        name="proj_ffn",
    )(x, y, wo, mod, g, wg, wu, wd)


def _ssd_in_kernel(x_ref, mod_ref, g_ref, wz_ref, wxbc_ref, wdt_ref, z_ref, xbc_ref, dt_ref):
    h = _modulated(x_ref[...], g_ref[...], mod_ref, 1).astype(BF16)
    z_ref[...] = _dot(h, wz_ref[...]).astype(BF16)
    xbc_ref[...] = _dot(h, wxbc_ref[...]).astype(BF16)
    dt_ref[...] = _dot(h, wdt_ref[...])


def _ssd_in_call(x, mod, g, wz, wxbc, wdt, *, seq):
    t, d = x.shape
    tm = ROW_TILE
    x_spec, mod_spec = _row_specs(t, d, seq, tm)
    widths = (wz.shape[1], wxbc.shape[1], wdt.shape[1])
    return pl.pallas_call(
        _ssd_in_kernel,
        grid=(t // tm,),
        in_specs=[x_spec, mod_spec, _resident((1, d)), _resident(wz.shape), _resident(wxbc.shape),
                  _resident(wdt.shape)],
        out_specs=[pl.BlockSpec((tm, w), lambda i: (i, 0)) for w in widths],
        out_shape=[jax.ShapeDtypeStruct((t, widths[0]), BF16),
                   jax.ShapeDtypeStruct((t, widths[1]), BF16),
                   jax.ShapeDtypeStruct((t, widths[2]), F32)],
        compiler_params=_params("arbitrary"),
        name="ssd_in",
    )(x, mod, g, wz, wxbc, wdt)


def _split3(v):
    v1 = v.astype(BF16)
    rem = v - v1.astype(F32)
    v2 = rem.astype(BF16)
    return v1, v2, (rem - v2.astype(F32)).astype(BF16)


def _ssd_dt_kernel(dtr_ref, dtb_ref, alog_ref, cum_ref, dtt_ref, cumt_ref):
    seq = dtr_ref.shape[0]
    q = SSD_CHUNK
    row_i = lax.broadcasted_iota(jnp.int32, (q, q), 0)
    col_j = lax.broadcasted_iota(jnp.int32, (q, q), 1)
    tri = jnp.where(row_i >= col_j, 1.0, 0.0).astype(BF16)
    lane = lax.broadcasted_iota(jnp.int32, (q, LANES), 1)
    is_fwd = (lane // SSD_HEADS_PER_GROUP) % 2 == 0
    a_coef = -jnp.exp(alog_ref[...]) * LOG2_E

    def chunk(c, carry):
        r0 = pl.multiple_of(c * q, q)
        dt = jax.nn.softplus(dtr_ref[pl.ds(r0, q), :] + dtb_ref[...])
        a = dt * a_coef
        a1, a2, a3 = _split3(a)
        prefix = _dot(tri, a1) + _dot(tri, a2) + _dot(tri, a3)
        cum = jnp.where(is_fwd, prefix, prefix[q - 1:q, :] - prefix + a)
        cum_ref[pl.ds(r0, q), :] = cum
        dtt_ref[:, pl.ds(r0, q)] = dt.T
        cumt_ref[:, pl.ds(r0, q)] = cum.T
        return carry

    lax.fori_loop(0, seq // q, chunk, 0, unroll=SSD_UNROLL)


def _ssd_dt_call(dt_raw, dt_bias, a_log, *, batch, seq):
    t = dt_raw.shape[0]
    head_major = pl.BlockSpec((None, LANES, seq), lambda b: (b, 0, 0))
    return pl.pallas_call(
        _ssd_dt_kernel,
        grid=(batch,),
        in_specs=[pl.BlockSpec((seq, LANES), lambda b: (b, 0)),
                  pl.BlockSpec((1, LANES), lambda b: (0, 0)),
                  pl.BlockSpec((1, LANES), lambda b: (0, 0))],
        out_specs=[pl.BlockSpec((seq, LANES), lambda b: (b, 0)), head_major, head_major],
        out_shape=[jax.ShapeDtypeStruct((t, LANES), F32),
                   jax.ShapeDtypeStruct((batch, LANES, seq), F32),
                   jax.ShapeDtypeStruct((batch, LANES, seq), F32)],
        compiler_params=_params("arbitrary"),
        name="ssd_dt",
    )(dt_raw, dt_bias, a_log)


def _ssd_core_kernel(xr_ref, br_ref, cr_ref, z_ref, cum_ref, dtt_ref, cumt_ref, cw_ref, cb_ref,
                     dsk_ref, ng_ref, o_ref, xt_ref, cs_ref, yt_ref, st_ref, dec_ref):
    seq = xr_ref.shape[0]
    q = SSD_CHUNK
    halo = BF16_SUBLANES
    nc = seq // q
    gl = GROUP_LANES
    n = SSD_STATE
    hp = SSD_HEAD_DIM
    heads = SSD_HEADS_PER_GROUP
    centre = SSD_CONV // 2
    group = pl.program_id(1)

    col_shift = jnp.where(group == 0, 0, LANES - 2 * heads * group)
    src_j = lax.broadcasted_iota(jnp.int32, (q, q), 0)
    dst_i = lax.broadcasted_iota(jnp.int32, (q, q), 1)
    causal = (dst_i >= src_j, dst_i <= src_j)
    tap_row = lax.broadcasted_iota(jnp.int32, (q, q + 2 * halo), 0)
    tap_col = lax.broadcasted_iota(jnp.int32, (q, q + 2 * halo), 1)
    shift_all = jnp.concatenate(
        [jnp.where(tap_col == tap_row + (halo + k - centre), 1.0, 0.0).astype(BF16) for k in range(SSD_CONV)],
        axis=1)
    cw_bf = cw_ref[...].astype(BF16)

    def rows_of(ref, r0, r1, r2):
        return jnp.concatenate([ref[pl.ds(r0, halo), :], ref[pl.ds(r1, q), :], ref[pl.ds(r2, halo), :]],
                               axis=0)

    def head_rows(vals, direction):
        return jnp.concatenate(
            [jnp.broadcast_to(vals[heads * direction + h:heads * direction + h + 1, :], (hp, vals.shape[1]))
             for h in range(heads)], axis=0)

    def conv_part(c):
        r0 = pl.multiple_of(c * q, q)
        rp = pl.multiple_of(jnp.maximum(r0 - halo, 0), halo)
        rn = pl.multiple_of(jnp.minimum(r0 + q, seq - halo), halo)
        win = jnp.concatenate([rows_of(xr_ref, rp, r0, rn), rows_of(br_ref, rp, r0, rn),
                               rows_of(cr_ref, rp, r0, rn)], axis=1)
        row = lax.broadcasted_iota(jnp.int32, (q + 2 * halo, 1), 0)
        outside = ((row < halo) & (c == 0)) | ((row >= halo + q) & (c == nc - 1))
        win = jnp.where(outside, jnp.zeros_like(win), win)
        weighted = jnp.concatenate([win * cw_bf[k:k + 1, :] for k in range(SSD_CONV)], axis=0)
        act = _silu(cb_ref[...] + _dot(shift_all, weighted))
        x = act[:, :gl]
        xt = jnp.concatenate([x[:, :LANES].T, x[:, LANES:].T], axis=0)
        b_bf = act[:, gl:gl + n].astype(BF16)
        c_bf = act[:, gl + n:].astype(BF16)
        xt_ref[c] = xt
        cs_ref[pl.ds(r0, q), :] = c_bf
        return xt, b_bf, c_bf

    def local_part(c, xt, b_bf, cbt):
        r0 = pl.multiple_of(c * q, q)
        cum_c = pltpu.roll(cum_ref[pl.ds(r0, q), :], col_shift, 1)
        dt_r = dtt_ref[:, pl.ds(r0, q)]
        cum_r = cumt_ref[:, pl.ds(r0, q)]
        total = jnp.concatenate([cum_r[:heads, q - 1:q], cum_r[heads:, 0:1]], axis=0)
        to_edge = jnp.exp2(total - cum_r)
        dec_ref[c] = jnp.broadcast_to(jnp.exp2(total), (2 * heads, n))
        xdt = [xt[(k % heads) * hp:(k % heads + 1) * hp, :] * dt_r[k:k + 1, :] for k in range(2 * heads)]
        st_ref[c] = _dot(jnp.concatenate([(xdt[k] * to_edge[k:k + 1, :]).astype(BF16)
                                          for k in range(2 * heads)], axis=0), b_bf)
        y_heads = []
        for h in range(heads):
            ks = (h, heads + h)
            lhs = jnp.concatenate([xdt[k].astype(BF16) for k in ks], axis=1)
            rhs = jnp.concatenate(
                [(cbt * jnp.exp2(jnp.where(causal[k // heads], cum_r[k:k + 1, :] - cum_c[:, k:k + 1],
                                           -jnp.inf))).astype(BF16) for k in ks], axis=0)
            y_heads.append(_dot(lhs, rhs))
        yt_ref[c] = jnp.concatenate(y_heads, axis=0)

    def phase_a(c, conv_c):
        xt, b_bf, c_bf = conv_c
        cbt = _dot_nt(b_bf, c_bf)
        conv_next = conv_part(c + 1)
        local_part(c, xt, b_bf, cbt)
        return conv_next

    xt, b_bf, c_bf = lax.fori_loop(0, nc - 1, phase_a, conv_part(0), unroll=SSD_UNROLL)
    local_part(nc - 1, xt, b_bf, _dot_nt(b_bf, c_bf))

    def step(t, carry):
        new = []
        for d in range(2):
            c = t if d == 0 else nc - 1 - t
            own = st_ref[c, d * gl:(d + 1) * gl]
            st_ref[c, d * gl:(d + 1) * gl] = carry[d]
            new.append(carry[d] * head_rows(dec_ref[c], d) + own)
        return tuple(new)

    lax.fori_loop(0, nc, step, (jnp.zeros((gl, n), F32), jnp.zeros((gl, n), F32)))

    def phase_c(c, carry):
        r0 = pl.multiple_of(c * q, q)
        grow = jnp.exp2(cumt_ref[:, pl.ds(r0, q)])
        c_bf = cs_ref[pl.ds(r0, q), :]
        off = _dot_nt(st_ref[c].astype(BF16), c_bf)
        yt = yt_ref[c] + xt_ref[c] * dsk_ref[...]
        for d in range(2):
            yt = yt + off[d * gl:(d + 1) * gl] * head_rows(grow, d)
        y = jnp.concatenate([yt[:LANES].T, yt[LANES:].T], axis=1)
        y = y * _silu(z_ref[pl.ds(r0, q), :].astype(F32))
        o_ref[pl.ds(r0, q), :] = (_rms(y) * ng_ref[...]).astype(BF16)
        return carry

    lax.fori_loop(0, nc, phase_c, 0, unroll=SSD_UNROLL)


def _ssd_core_call(z, xbc, cum, dtt, cumt, conv_w_g, conv_b_g, d_skip_col, norm_g, *, batch, seq):
    t, d_inner = z.shape
    n = SSD_STATE
    gl = GROUP_LANES
    x_blocks = d_inner // n
    c_first = x_blocks + SSD_GROUPS
    nc = seq // SSD_CHUNK
    dir_heads = 2 * SSD_HEADS_PER_GROUP
    conv_lanes = conv_w_g.shape[-1]
    in_specs = [
        pl.BlockSpec((seq, gl), lambda b, g: (b, g)),
        pl.BlockSpec((seq, n), lambda b, g: (b, x_blocks + g)),
        pl.BlockSpec((seq, n), lambda b, g: (b, c_first + g)),
        pl.BlockSpec((seq, gl), lambda b, g: (b, g)),
        pl.BlockSpec((seq, LANES), lambda b, g: (b, 0)),
        pl.BlockSpec((None, dir_heads, seq), lambda b, g: (b, g, 0)),
        pl.BlockSpec((None, dir_heads, seq), lambda b, g: (b, g, 0)),
        pl.BlockSpec((None, SSD_CONV, conv_lanes), lambda b, g: (g, 0, 0)),
        pl.BlockSpec((None, 1, conv_lanes), lambda b, g: (g, 0, 0)),
        pl.BlockSpec((gl, LANES), lambda b, g: (g, 0)),
        pl.BlockSpec((1, gl), lambda b, g: (0, g)),
    ]
    return pl.pallas_call(
        _ssd_core_kernel,
        grid=(batch, SSD_GROUPS),
        in_specs=in_specs,
        out_specs=pl.BlockSpec((seq, gl), lambda b, g: (b, g)),
        out_shape=jax.ShapeDtypeStruct((t, d_inner), BF16),
        scratch_shapes=[
            pltpu.VMEM((nc, gl, SSD_CHUNK), F32),
            pltpu.VMEM((seq, n), BF16),
            pltpu.VMEM((nc, gl, SSD_CHUNK), F32),
            pltpu.VMEM((nc, 2 * gl, n), F32),
            pltpu.VMEM((nc, dir_heads, n), F32),
        ],
        compiler_params=_params("arbitrary", "arbitrary"),
        name="ssd_core",
    )(xbc, xbc, xbc, z, cum, dtt, cumt, conv_w_g, conv_b_g, d_skip_col, norm_g)


def _mla_in_kernel(x_ref, pos_ref, mod_ref, g_ref, win_ref, qng_ref, kvng_ref, wuq_ref, wk_ref, wv_ref,
                   qhg_ref, khg_ref, freq_ref, vone_ref, q_ref, k_ref, v_ref):
    hl = MLA_HEAD_LANES
    tm = x_ref.shape[0]
    half = MLA_ROPE // 2
    h = _modulated(x_ref[...], g_ref[...], mod_ref, 1).astype(BF16)
    lat = _dot(h, win_ref[...])
    q_lat = (_rms(lat[:, :MLA_Q_LORA]) * qng_ref[...]).astype(BF16)
    kv_lat = (_rms(lat[:, MLA_Q_LORA:MLA_Q_LORA + MLA_KV_LORA]) * kvng_ref[...]).astype(BF16)
    k_pe = lat[:, MLA_Q_LORA + MLA_KV_LORA:]
    qf = _dot(q_lat, wuq_ref[...])
    kf = _dot(kv_lat, wk_ref[...])
    v_ref[...] = (_dot(kv_lat, wv_ref[...]) + vone_ref[...]).astype(BF16)

    ang = freq_ref[...] * pos_ref[...]
    cos_f, sin_f = jnp.cos(ang), jnp.sin(ang)
    rest = hl // 2 - half
    cos_t = jnp.concatenate([cos_f, jnp.ones((rest, tm), F32)] * 2, axis=0)
    sin_t = jnp.concatenate([-sin_f, jnp.zeros((rest, tm), F32), sin_f, jnp.zeros((rest, tm), F32)], axis=0)

    def token_major(tab):
        return jnp.concatenate([tab[:, r:r + LANES].T for r in range(0, tm, LANES)], axis=0)

    cos = jnp.concatenate([token_major(cos_t)] * 2, axis=1)
    sin = jnp.concatenate([token_major(sin_t)] * 2, axis=1)
    q_scale = MLA_QK ** -0.5 * LOG2_E
    blk_r = lax.broadcasted_iota(jnp.int32, (2 * hl, 2 * hl), 0) // hl
    blk_c = lax.broadcasted_iota(jnp.int32, (2 * hl, 2 * hl), 1) // hl
    head_ones = jnp.where(blk_r == blk_c, 1.0, 0.0).astype(BF16)

    def pair_norm_rope(t, gain):
        sq = _dot((t * t).astype(BF16), head_ones)
        t = t * lax.rsqrt(sq + MLA_QK * NORM_EPS) * gain
        partner = jnp.concatenate([pltpu.roll(t[:, :hl], hl // 2, 1), pltpu.roll(t[:, hl:], hl // 2, 1)], axis=1)
        return t * cos + partner * sin

    q_gain = jnp.concatenate([qhg_ref[...]] * 2, axis=1) * (MLA_QK ** 0.5 * q_scale)
    k_gain = jnp.concatenate([khg_ref[...]] * 2, axis=1) * (MLA_QK ** 0.5)
    k_pe2 = jnp.concatenate([k_pe] * 2, axis=1)
    for pr in range(MLA_HEADS // 2):
        sl = slice(2 * pr * hl, 2 * (pr + 1) * hl)
        q_ref[:, sl] = pair_norm_rope(qf[:, sl], q_gain).astype(BF16)
        k_ref[:, sl] = pair_norm_rope(kf[:, sl] + k_pe2, k_gain).astype(BF16)


def _mla_in_call(x, pos, mod, g, win, qng, kvng, wuq, wk, wv, qhg, khg, freq, vone, *, seq):
    t, d = x.shape
    tm = ROW_TILE
    x_spec, mod_spec = _row_specs(t, d, seq, tm)
    qk_w = MLA_HEADS * MLA_HEAD_LANES
    v_w = qk_w
    return pl.pallas_call(
        _mla_in_kernel,
        grid=(t // tm,),
        in_specs=[x_spec, pl.BlockSpec((None, 1, tm), lambda i: (i, 0, 0)), mod_spec, _resident((1, d)),
                  _resident(win.shape), _resident(qng.shape), _resident(kvng.shape),
                  _resident(wuq.shape), _resident(wk.shape), _resident(wv.shape),
                  _resident(qhg.shape), _resident(khg.shape), _resident(freq.shape),
                  _resident(vone.shape)],
        out_specs=[pl.BlockSpec((tm, qk_w), lambda i: (i, 0)),
                   pl.BlockSpec((tm, qk_w), lambda i: (i, 0)),
                   pl.BlockSpec((tm, v_w), lambda i: (i, 0))],
        out_shape=[jax.ShapeDtypeStruct((t, qk_w), BF16),
                   jax.ShapeDtypeStruct((t, qk_w), BF16),
                   jax.ShapeDtypeStruct((t, v_w), BF16)],
        compiler_params=_params("arbitrary"),
        name="mla_in",
    )(x, pos.reshape(t // tm, 1, tm), mod, g, win, qng, kvng, wuq, wk, wv, qhg, khg, freq, vone)


def _attn_kernel(q_ref, k_ref, v_ref, o_ref):
    hl = MLA_HEAD_LANES
    rows = q_ref.shape[0] // ATTN_ROW_SPLIT
    lane = lax.broadcasted_iota(jnp.int32, (rows, o_ref.shape[1]), 1)
    items = [(part, hd) for part in range(ATTN_ROW_SPLIT) for hd in range(2)]
    s = [_dot_nt(q_ref[part * rows:(part + 1) * rows, hd * hl:(hd + 1) * hl],
                 k_ref[:, hd * hl:(hd + 1) * hl]) for part, hd in items]
    outs = []
    for idx, (part, hd) in enumerate(items):
        p = jnp.exp2(s[idx] - jnp.max(s[idx], axis=-1, keepdims=True))
        r = _dot(p.astype(BF16), v_ref[:, hd * hl:(hd + 1) * hl])
        outs.append(r / r[:, MLA_V:MLA_V + 1])
    for part in range(ATTN_ROW_SPLIT):
        second = pltpu.roll(outs[2 * part + 1], MLA_V, 1)
        out = jnp.where(lane >= MLA_V, second, outs[2 * part])
        o_ref[part * rows:(part + 1) * rows, :] = out.astype(BF16)


def _attn_call(q, k, v, *, batch, seq):
    tq = ATTN_Q_TILE
    nq = seq // tq
    pair_qk = 2 * MLA_HEAD_LANES
    pair_v = 2 * MLA_V
    return pl.pallas_call(
        _attn_kernel,
        grid=(batch, MLA_HEADS // 2, nq),
        in_specs=[pl.BlockSpec((tq, pair_qk), lambda b, p, i: (b * nq + i, p)),
                  pl.BlockSpec((seq, pair_qk), lambda b, p, i: (b, p)),
                  pl.BlockSpec((seq, pair_qk), lambda b, p, i: (b, p))],
        out_specs=pl.BlockSpec((tq, pair_v), lambda b, p, i: (b * nq + i, p)),
        out_shape=jax.ShapeDtypeStruct((q.shape[0], MLA_HEADS * MLA_V), BF16),
        compiler_params=_params("arbitrary", "arbitrary", "arbitrary"),
        name="attn",
    )(q, k, v)


def _mla_head_perm():
    half = MLA_ROPE // 2
    perm = [-1] * MLA_HEAD_LANES
    for l in range(half):
        perm[l] = MLA_NOPE + l
        perm[MLA_HEAD_LANES // 2 + l] = MLA_NOPE + half + l
    nope_lo = MLA_HEAD_LANES // 2 - half
    for l in range(nope_lo):
        perm[half + l] = l
    for l in range(MLA_NOPE - nope_lo):
        perm[MLA_HEAD_LANES // 2 + half + l] = nope_lo + l
    return perm


def _gather_cols(w, perm):
    idx = jnp.array([max(p, 0) for p in perm], jnp.int32)
    keep = jnp.array([p >= 0 for p in perm])
    return jnp.where(keep, jnp.take(w, idx, axis=-1), 0.0)


def _mla_layout(w_in, w_uq, w_ukv, q_head_g, k_head_g):
    perm = _mla_head_perm()
    half = MLA_ROPE // 2
    lat = MLA_Q_LORA + MLA_KV_LORA
    pe_perm = [p - MLA_NOPE if p >= MLA_NOPE else -1 for p in perm]
    win = jnp.concatenate([w_in[:, :lat], _gather_cols(w_in[:, lat:], pe_perm)], axis=1)
    wuq = _gather_cols(w_uq.reshape(MLA_Q_LORA, MLA_HEADS, MLA_QK), perm)
    wuq = wuq.reshape(MLA_Q_LORA, MLA_HEADS * MLA_HEAD_LANES)
    kv = w_ukv.reshape(MLA_KV_LORA, MLA_HEADS, MLA_NOPE + MLA_V)
    nope_perm = [p if 0 <= p < MLA_NOPE else -1 for p in perm]
    wk = _gather_cols(kv[:, :, :MLA_NOPE], nope_perm).reshape(MLA_KV_LORA, MLA_HEADS * MLA_HEAD_LANES)
    v_pad = [(0, 0), (0, 0), (0, MLA_HEAD_LANES - MLA_V)]
    wv = jnp.pad(kv[:, :, MLA_NOPE:], v_pad).reshape(MLA_KV_LORA, MLA_HEADS * MLA_HEAD_LANES)
    vone = jnp.tile((jnp.arange(MLA_HEAD_LANES) == MLA_V).astype(F32), MLA_HEADS)[None, :]
    qhg = _gather_cols(q_head_g[None, :], perm)
    khg = _gather_cols(k_head_g[None, :], perm)
    freq = (ROPE_THETA ** (-jnp.arange(0, MLA_ROPE, 2, dtype=F32) / MLA_ROPE))[:, None]
    return (win.astype(BF16), wuq.astype(BF16), wk.astype(BF16), wv.astype(BF16), qhg, khg, freq, vone)


def _ssd_layout(w_in, conv_w, conv_b, dt_bias, a_log, d_skip):
    d_inner = SSD_GROUPS * GROUP_LANES
    bc_ch = SSD_GROUPS * SSD_STATE
    conv_ch = d_inner + 2 * bc_ch
    heads = SSD_GROUPS * SSD_HEADS_PER_GROUP
    wz = w_in[:, :d_inner].astype(BF16)
    wxbc = w_in[:, d_inner:d_inner + conv_ch].astype(BF16)

    def conv_by_group(v):
        rows = v.shape[0]
        parts = [v[:, :d_inner].reshape(rows, SSD_GROUPS, GROUP_LANES),
                 v[:, d_inner:d_inner + bc_ch].reshape(rows, SSD_GROUPS, SSD_STATE),
                 v[:, d_inner + bc_ch:].reshape(rows, SSD_GROUPS, SSD_STATE)]
        return jnp.swapaxes(jnp.concatenate(parts, axis=-1), 0, 1)

    def by_group(v):
        lead = v.shape[:-1]
        v = v.reshape(lead + (2, SSD_GROUPS, SSD_HEADS_PER_GROUP))
        v = jnp.swapaxes(v, -3, -2).reshape(lead + (2 * heads,))
        return jnp.pad(v, [(0, 0)] * len(lead) + [(0, LANES - 2 * heads)])

    wdt = by_group(w_in[:, d_inner + conv_ch:]).astype(BF16)
    dtb = by_group(dt_bias.reshape(1, 2 * heads))
    a_log_cols = by_group(a_log.reshape(1, 2 * heads))
    d_skip_col = jnp.broadcast_to(jnp.repeat(d_skip, SSD_HEAD_DIM)[:, None], (d_inner, LANES))
    return wz, wxbc, wdt, dtb, a_log_cols, d_skip_col, conv_by_group(conv_w), conv_by_group(conv_b[None, :])


def kernel(x, c, positions, norm_g, w_mod, b_mod, ffn_w_gate, ffn_w_up, ffn_w_down, ssd_w_in, ssd_conv_w,
           ssd_conv_b, ssd_dt_bias, ssd_a_log, ssd_d, ssd_norm_g, ssd_w_out, mla_w_in, mla_q_norm_g,
           mla_kv_norm_g, mla_w_uq, mla_w_ukv, mla_q_head_g, mla_k_head_g, mla_w_out):
    batch, seq, d = x.shape
    depth = w_mod.shape[0]
    t = batch * seq
    n_mixers = 2
    xf = x.reshape(t, d)
    pos = positions.astype(F32).reshape(t, 1)
    mod = _mod_call(c, w_mod, b_mod).reshape(depth, batch, 3 * N_SUBLAYERS, d)
    wg, wu, wd = (w.astype(BF16) for w in (ffn_w_gate, ffn_w_up, ffn_w_down))

    for i in range(depth):
        g = norm_g[i][:, None, :]
        xf = _ffn_call(xf, mod[i], g[0], wg[i, 0], wu[i, 0], wd[i, 0], sub=0, seq=seq)
        j = i // n_mixers
        if i % n_mixers == 0:
            wz, wxbc, wdt, dtb, a_log_cols, d_skip_col, conv_w_g, conv_b_g = _ssd_layout(
                ssd_w_in[j], ssd_conv_w[j], ssd_conv_b[j], ssd_dt_bias[j], ssd_a_log[j], ssd_d[j])
            z, xbc, dt_raw = _ssd_in_call(xf, mod[i], g[1], wz, wxbc, wdt, seq=seq)
            cum, dtt, cumt = _ssd_dt_call(dt_raw, dtb, a_log_cols, batch=batch, seq=seq)
            y = _ssd_core_call(z, xbc, cum, dtt, cumt, conv_w_g, conv_b_g, d_skip_col,
                               ssd_norm_g[j][None, :], batch=batch, seq=seq)
            w_out = ssd_w_out[j].astype(BF16)
        else:
            win, wuq, wk, wv, qhg, khg, freq, vone = _mla_layout(
                mla_w_in[j], mla_w_uq[j], mla_w_ukv[j], mla_q_head_g[j], mla_k_head_g[j])
            q, k, v = _mla_in_call(xf, pos, mod[i], g[1], win, mla_q_norm_g[j][None, :],
                                   mla_kv_norm_g[j][None, :], wuq, wk, wv, qhg, khg, freq, vone, seq=seq)
            y = _attn_call(q, k, v, batch=batch, seq=seq)
            w_out = mla_w_out[j].astype(BF16)
        xf = _proj_ffn_call(xf, y, w_out, mod[i], g[2], wg[i, 1], wu[i, 1], wd[i, 1], seq=seq)
    return xf.reshape(batch, seq, d)
```

```python
import functools

import jax
import jax.numpy as jnp
from jax import lax
from jax.experimental import pallas as pl
from jax.experimental.pallas import tpu as pltpu

F32 = jnp.float32
BF16 = jnp.bfloat16

N_SUBLAYERS = 3
FFN_RES_WEIGHT = 0.5
NORM_EPS = 1e-6
SSD_HEAD_DIM = 64
SSD_GROUPS = 8
SSD_HEADS_PER_GROUP = 4
SSD_STATE = 128
SSD_CONV = 5
SSD_CHUNK = 128
MLA_HEADS = 16
MLA_Q_LORA = 384
MLA_KV_LORA = 256
MLA_NOPE = 64
MLA_ROPE = 32
MLA_V = 64
MLA_QK = MLA_NOPE + MLA_ROPE
ROPE_THETA = 10000.0
LOG2_E = 1.4426950408889634

LANES = 128
BF16_SUBLANES = 16
VMEM_LIMIT_BYTES = 56 * 1024 * 1024

ROW_TILE = 512
MOD_COL_BLOCKS = 8
ATTN_Q_TILE = 1024
ATTN_ROW_SPLIT = 2
MLA_HEAD_LANES = LANES
GROUP_LANES = SSD_HEADS_PER_GROUP * SSD_HEAD_DIM
SSD_UNROLL = 4


def _params(*semantics):
    return pltpu.CompilerParams(dimension_semantics=semantics,
                                vmem_limit_bytes=VMEM_LIMIT_BYTES)


def _silu(v):
    return v * jax.nn.sigmoid(v)


def _rms(v):
    ms = jnp.mean(v * v, axis=-1, keepdims=True)
    return v * lax.rsqrt(ms + NORM_EPS)


def _modulated(x, g, mod_ref, sub):
    shift = mod_ref[3 * sub:3 * sub + 1, :]
    scale = mod_ref[3 * sub + 1:3 * sub + 2, :]
    return _rms(x) * g * (1.0 + scale) + shift


def _dot(a, b):
    return jnp.dot(a, b, preferred_element_type=F32)


def _dot_nt(a, b):
    return lax.dot_general(a, b, (((1,), (1,)), ((), ())), preferred_element_type=F32)


def _mod_kernel(c_ref, w_ref, b_ref, o_ref):
    o_ref[...] = _dot(_silu(c_ref[...]), w_ref[...]) + b_ref[...]


def _mod_call(c, w_mod, b_mod):
    depth, d, n = w_mod.shape
    bsz = c.shape[0]
    tn = n // MOD_COL_BLOCKS
    return pl.pallas_call(
        _mod_kernel,
        grid=(depth, n // tn),
        in_specs=[
            pl.BlockSpec((bsz, d), lambda i, j: (0, 0)),
            pl.BlockSpec((None, d, tn), lambda i, j: (i, 0, j)),
            pl.BlockSpec((None, 1, tn), lambda i, j: (i, 0, j)),
        ],
        out_specs=pl.BlockSpec((None, bsz, tn), lambda i, j: (i, 0, j)),
        out_shape=jax.ShapeDtypeStruct((depth, bsz, n), F32),
        compiler_params=_params("arbitrary", "arbitrary"),
        name="mod",
    )(c, w_mod, b_mod.reshape(depth, 1, n))


def _ff_chunks(d_ff):
    half = (d_ff // 512) * 256
    return ((0, half), (half, d_ff)) if 0 < half < d_ff else ((0, d_ff),)


def _ffn_apply(x, mod_ref, g_ref, wg_ref, wu_ref, wd_ref, sub):
    h = _modulated(x, g_ref[...], mod_ref, sub).astype(BF16)
    acc = None
    for lo, hi in _ff_chunks(wg_ref.shape[1]):
        gate = _dot(h, wg_ref[:, lo:hi])
        up = _dot(h, wu_ref[:, lo:hi])
        part = _dot((_silu(gate) * up).astype(BF16), wd_ref[lo:hi, :])
        acc = part if acc is None else acc + part
    res_gate = mod_ref[3 * sub + 2:3 * sub + 3, :]
    return x + (FFN_RES_WEIGHT * res_gate) * acc


def _ffn_kernel(x_ref, mod_ref, g_ref, wg_ref, wu_ref, wd_ref, o_ref, *, sub):
    o_ref[...] = _ffn_apply(x_ref[...], mod_ref, g_ref, wg_ref, wu_ref, wd_ref, sub)


def _proj_ffn_kernel(x_ref, y_ref, wo_ref, mod_ref, g_ref, wg_ref, wu_ref, wd_ref, o_ref):
    mix_gate = mod_ref[3 * 1 + 2:3 * 1 + 3, :]
    x = x_ref[...] + mix_gate * _dot(y_ref[...], wo_ref[...])
    o_ref[...] = _ffn_apply(x, mod_ref, g_ref, wg_ref, wu_ref, wd_ref, 2)


def _resident(shape):
    return pl.BlockSpec(shape, lambda *_: (0,) * len(shape), pipeline_mode=pl.Buffered(1))


def _row_specs(t, d, seq, tm):
    tiles_per_seq = seq // tm
    x_spec = pl.BlockSpec((tm, d), lambda i: (i, 0))
    mod_spec = pl.BlockSpec((None, 3 * N_SUBLAYERS, d), lambda i: (i // tiles_per_seq, 0, 0))
    return x_spec, mod_spec


def _ffn_call(x, mod, g, wg, wu, wd, *, sub, seq):
    t, d = x.shape
    tm = ROW_TILE
    x_spec, mod_spec = _row_specs(t, d, seq, tm)
    return pl.pallas_call(
        functools.partial(_ffn_kernel, sub=sub),
        grid=(t // tm,),
        in_specs=[x_spec, mod_spec, _resident((1, d)), _resident(wg.shape), _resident(wu.shape),
                  _resident(wd.shape)],
        out_specs=x_spec,
        out_shape=jax.ShapeDtypeStruct((t, d), F32),
        compiler_params=_params("arbitrary"),
        name="ffn",
    )(x, mod, g, wg, wu, wd)


def _proj_ffn_call(x, y, wo, mod, g, wg, wu, wd, *, seq):
    t, d = x.shape
    tm = ROW_TILE
    x_spec, mod_spec = _row_specs(t, d, seq, tm)
    y_spec = pl.BlockSpec((tm, y.shape[1]), lambda i: (i, 0))
    return pl.pallas_call(
        _proj_ffn_kernel,
        grid=(t // tm,),
        in_specs=[x_spec, y_spec, _resident(wo.shape), mod_spec, _resident((1, d)), _resident(wg.shape),
                  _resident(wu.shape), _resident(wd.shape)],
        out_specs=x_spec,
        out_shape=jax.ShapeDtypeStruct((t, d), F32),
        compiler_params=_params("arbitrary"),
        name="proj_ffn",
    )(x, y, wo, mod, g, wg, wu, wd)


def _ssd_in_kernel(x_ref, mod_ref, g_ref, wz_ref, wxbc_ref, wdt_ref, z_ref, xbc_ref, dt_ref):
    h = _modulated(x_ref[...], g_ref[...], mod_ref, 1).astype(BF16)
    z_ref[...] = _dot(h, wz_ref[...]).astype(BF16)
    xbc_ref[...] = _dot(h, wxbc_ref[...]).astype(BF16)
    dt_ref[...] = _dot(h, wdt_ref[...])


def _ssd_in_call(x, mod, g, wz, wxbc, wdt, *, seq):
    t, d = x.shape
    tm = ROW_TILE
    x_spec, mod_spec = _row_specs(t, d, seq, tm)
    widths = (wz.shape[1], wxbc.shape[1], wdt.shape[1])
    return pl.pallas_call(
        _ssd_in_kernel,
        grid=(t // tm,),
        in_specs=[x_spec, mod_spec, _resident((1, d)), _resident(wz.shape), _resident(wxbc.shape),
                  _resident(wdt.shape)],
        out_specs=[pl.BlockSpec((tm, w), lambda i: (i, 0)) for w in widths],
        out_shape=[jax.ShapeDtypeStruct((t, widths[0]), BF16),
                   jax.ShapeDtypeStruct((t, widths[1]), BF16),
                   jax.ShapeDtypeStruct((t, widths[2]), F32)],
        compiler_params=_params("arbitrary"),
        name="ssd_in",
    )(x, mod, g, wz, wxbc, wdt)


def _split3(v):
    v1 = v.astype(BF16)
    rem = v - v1.astype(F32)
    v2 = rem.astype(BF16)
    return v1, v2, (rem - v2.astype(F32)).astype(BF16)


def _ssd_dt_kernel(dtr_ref, dtb_ref, alog_ref, cum_ref, dtt_ref, cumt_ref):
    seq = dtr_ref.shape[0]
    q = SSD_CHUNK
    row_i = lax.broadcasted_iota(jnp.int32, (q, q), 0)
    col_j = lax.broadcasted_iota(jnp.int32, (q, q), 1)
    tri = jnp.where(row_i >= col_j, 1.0, 0.0).astype(BF16)
    lane = lax.broadcasted_iota(jnp.int32, (q, LANES), 1)
    is_fwd = (lane // SSD_HEADS_PER_GROUP) % 2 == 0
    a_coef = -jnp.exp(alog_ref[...]) * LOG2_E

    def chunk(c, carry):
        r0 = pl.multiple_of(c * q, q)
        dt = jax.nn.softplus(dtr_ref[pl.ds(r0, q), :] + dtb_ref[...])
        a = dt * a_coef
        a1, a2, a3 = _split3(a)
        prefix = _dot(tri, a1) + _dot(tri, a2) + _dot(tri, a3)
        cum = jnp.where(is_fwd, prefix, prefix[q - 1:q, :] - prefix + a)
        cum_ref[pl.ds(r0, q), :] = cum
        dtt_ref[:, pl.ds(r0, q)] = dt.T
        cumt_ref[:, pl.ds(r0, q)] = cum.T
        return carry

    lax.fori_loop(0, seq // q, chunk, 0, unroll=SSD_UNROLL)


def _ssd_dt_call(dt_raw, dt_bias, a_log, *, batch, seq):
    t = dt_raw.shape[0]
    head_major = pl.BlockSpec((None, LANES, seq), lambda b: (b, 0, 0))
    return pl.pallas_call(
        _ssd_dt_kernel,
        grid=(batch,),
        in_specs=[pl.BlockSpec((seq, LANES), lambda b: (b, 0)),
                  pl.BlockSpec((1, LANES), lambda b: (0, 0)),
                  pl.BlockSpec((1, LANES), lambda b: (0, 0))],
        out_specs=[pl.BlockSpec((seq, LANES), lambda b: (b, 0)), head_major, head_major],
        out_shape=[jax.ShapeDtypeStruct((t, LANES), F32),
                   jax.ShapeDtypeStruct((batch, LANES, seq), F32),
                   jax.ShapeDtypeStruct((batch, LANES, seq), F32)],
        compiler_params=_params("arbitrary"),
        name="ssd_dt",
    )(dt_raw, dt_bias, a_log)


def _ssd_core_kernel(xr_ref, br_ref, cr_ref, z_ref, cum_ref, dtt_ref, cumt_ref, cw_ref, cb_ref,
                     dsk_ref, ng_ref, o_ref, xt_ref, cs_ref, yt_ref, st_ref, dec_ref):
    seq = xr_ref.shape[0]
    q = SSD_CHUNK
    halo = BF16_SUBLANES
    nc = seq // q
    gl = GROUP_LANES
    n = SSD_STATE
    hp = SSD_HEAD_DIM
    heads = SSD_HEADS_PER_GROUP
    centre = SSD_CONV // 2
    group = pl.program_id(1)

    col_shift = jnp.where(group == 0, 0, LANES - 2 * heads * group)
    src_j = lax.broadcasted_iota(jnp.int32, (q, q), 0)
    dst_i = lax.broadcasted_iota(jnp.int32, (q, q), 1)
    causal = (dst_i >= src_j, dst_i <= src_j)
    def shift_block(k):
        rows = q if k == centre else q + halo
        first = (halo if k < centre else 0) + k - centre
        out_row = lax.broadcasted_iota(jnp.int32, (q, rows), 0)
        win_row = lax.broadcasted_iota(jnp.int32, (q, rows), 1)
        return jnp.where(win_row == out_row + first, 1.0, 0.0).astype(BF16)

    shift_all = jnp.concatenate([shift_block(k) for k in range(SSD_CONV)], axis=1)
    cw_bf = cw_ref[...].astype(BF16)

    def lanes_of(r0, rows):
        return jnp.concatenate([xr_ref[pl.ds(r0, rows), :], br_ref[pl.ds(r0, rows), :],
                                cr_ref[pl.ds(r0, rows), :]], axis=1)

    def head_rows(vals, direction):
        return jnp.concatenate(
            [jnp.broadcast_to(vals[heads * direction + h:heads * direction + h + 1, :], (hp, vals.shape[1]))
             for h in range(heads)], axis=0)

    def conv_part(c):
        r0 = pl.multiple_of(c * q, q)
        rp = pl.multiple_of(jnp.maximum(r0 - halo, 0), halo)
        rn = pl.multiple_of(jnp.minimum(r0 + q, seq - halo), halo)
        cur = lanes_of(r0, q)
        prev = lanes_of(rp, halo)
        nxt = lanes_of(rn, halo)
        prev = jnp.where(c == 0, jnp.zeros_like(prev), prev)
        nxt = jnp.where(c == nc - 1, jnp.zeros_like(nxt), nxt)
        weighted = []
        for k in range(SSD_CONV):
            w_k = cw_bf[k:k + 1, :]
            if k < centre:
                weighted += [prev * w_k, cur * w_k]
            elif k > centre:
                weighted += [cur * w_k, nxt * w_k]
            else:
                weighted.append(cur * w_k)
        act = _silu(cb_ref[...] + _dot(shift_all, jnp.concatenate(weighted, axis=0)))
        x = act[:, :gl]
        xt = jnp.concatenate([x[:, :LANES].T, x[:, LANES:].T], axis=0)
        b_bf = act[:, gl:gl + n].astype(BF16)
        c_bf = act[:, gl + n:].astype(BF16)
        xt_ref[c] = xt
        cs_ref[pl.ds(r0, q), :] = c_bf
        return xt, b_bf, c_bf

    def local_part(c, xt, b_bf, cbt):
        r0 = pl.multiple_of(c * q, q)
        cum_c = pltpu.roll(cum_ref[pl.ds(r0, q), :], col_shift, 1)
        dt_r = dtt_ref[:, pl.ds(r0, q)]
        cum_r = cumt_ref[:, pl.ds(r0, q)]
        total = jnp.concatenate([cum_r[:heads, q - 1:q], cum_r[heads:, 0:1]], axis=0)
        to_edge = jnp.exp2(total - cum_r)
        dec_ref[c] = jnp.broadcast_to(jnp.exp2(total), (2 * heads, n))
        xdt = [xt[(k % heads) * hp:(k % heads + 1) * hp, :] * dt_r[k:k + 1, :] for k in range(2 * heads)]
        st_ref[c] = _dot(jnp.concatenate([(xdt[k] * to_edge[k:k + 1, :]).astype(BF16)
                                          for k in range(2 * heads)], axis=0), b_bf)
        y_heads = []
        for h in range(heads):
            ks = (h, heads + h)
            lhs = jnp.concatenate([xdt[k].astype(BF16) for k in ks], axis=1)
            rhs = jnp.concatenate(
                [(cbt * jnp.exp2(jnp.where(causal[k // heads], cum_r[k:k + 1, :] - cum_c[:, k:k + 1],
                                           -jnp.inf))).astype(BF16) for k in ks], axis=0)
            y_heads.append(_dot(lhs, rhs))
        yt_ref[c] = jnp.concatenate(y_heads, axis=0)

    def phase_a(c, conv_c):
        xt, b_bf, c_bf = conv_c
        cbt = _dot_nt(b_bf, c_bf)
        conv_next = conv_part(jnp.minimum(c + 1, nc - 1))
        local_part(c, xt, b_bf, cbt)
        return conv_next

    lax.fori_loop(0, nc, phase_a, conv_part(0), unroll=SSD_UNROLL)

    def step(t, carry):
        new = []
        for d in range(2):
            c = t if d == 0 else nc - 1 - t
            own = st_ref[c, d * gl:(d + 1) * gl]
            st_ref[c, d * gl:(d + 1) * gl] = carry[d]
            new.append(carry[d] * head_rows(dec_ref[c], d) + own)
        return tuple(new)

    lax.fori_loop(0, nc, step, (jnp.zeros((gl, n), F32), jnp.zeros((gl, n), F32)))

    def phase_c(c, carry):
        r0 = pl.multiple_of(c * q, q)
        grow = jnp.exp2(cumt_ref[:, pl.ds(r0, q)])
        c_bf = cs_ref[pl.ds(r0, q), :]
        off = _dot_nt(st_ref[c].astype(BF16), c_bf)
        yt = yt_ref[c] + xt_ref[c] * dsk_ref[...]
        for d in range(2):
            yt = yt + off[d * gl:(d + 1) * gl] * head_rows(grow, d)
        y = jnp.concatenate([yt[:LANES].T, yt[LANES:].T], axis=1)
        y = y * _silu(z_ref[pl.ds(r0, q), :].astype(F32))
        o_ref[pl.ds(r0, q), :] = (_rms(y) * ng_ref[...]).astype(BF16)
        return carry

    lax.fori_loop(0, nc, phase_c, 0, unroll=SSD_UNROLL)


def _ssd_core_call(z, xbc, cum, dtt, cumt, conv_w_g, conv_b_g, d_skip_col, norm_g, *, batch, seq):
    t, d_inner = z.shape
    n = SSD_STATE
    gl = GROUP_LANES
    x_blocks = d_inner // n
    c_first = x_blocks + SSD_GROUPS
    nc = seq // SSD_CHUNK
    dir_heads = 2 * SSD_HEADS_PER_GROUP
    conv_lanes = conv_w_g.shape[-1]
    in_specs = [
        pl.BlockSpec((seq, gl), lambda b, g: (b, g)),
        pl.BlockSpec((seq, n), lambda b, g: (b, x_blocks + g)),
        pl.BlockSpec((seq, n), lambda b, g: (b, c_first + g)),
        pl.BlockSpec((seq, gl), lambda b, g: (b, g)),
        pl.BlockSpec((seq, LANES), lambda b, g: (b, 0)),
        pl.BlockSpec((None, dir_heads, seq), lambda b, g: (b, g, 0)),
        pl.BlockSpec((None, dir_heads, seq), lambda b, g: (b, g, 0)),
        pl.BlockSpec((None, SSD_CONV, conv_lanes), lambda b, g: (g, 0, 0)),
        pl.BlockSpec((None, 1, conv_lanes), lambda b, g: (g, 0, 0)),
        pl.BlockSpec((gl, LANES), lambda b, g: (g, 0)),
        pl.BlockSpec((1, gl), lambda b, g: (0, g)),
    ]
    return pl.pallas_call(
        _ssd_core_kernel,
        grid=(batch, SSD_GROUPS),
        in_specs=in_specs,
        out_specs=pl.BlockSpec((seq, gl), lambda b, g: (b, g)),
        out_shape=jax.ShapeDtypeStruct((t, d_inner), BF16),
        scratch_shapes=[
            pltpu.VMEM((nc, gl, SSD_CHUNK), F32),
            pltpu.VMEM((seq, n), BF16),
            pltpu.VMEM((nc, gl, SSD_CHUNK), F32),
            pltpu.VMEM((nc, 2 * gl, n), F32),
            pltpu.VMEM((nc, dir_heads, n), F32),
        ],
        compiler_params=_params("arbitrary", "arbitrary"),
        name="ssd_core",
    )(xbc, xbc, xbc, z, cum, dtt, cumt, conv_w_g, conv_b_g, d_skip_col, norm_g)


def _mla_in_kernel(x_ref, pos_ref, mod_ref, g_ref, win_ref, qng_ref, kvng_ref, wuq_ref, wk_ref, wv_ref,
                   qhg_ref, khg_ref, freq_ref, vone_ref, q_ref, k_ref, v_ref):
    hl = MLA_HEAD_LANES
    tm = x_ref.shape[0]
    half = MLA_ROPE // 2
    h = _modulated(x_ref[...], g_ref[...], mod_ref, 1).astype(BF16)
    lat = _dot(h, win_ref[...])
    q_lat = (_rms(lat[:, :MLA_Q_LORA]) * qng_ref[...]).astype(BF16)
    kv_lat = (_rms(lat[:, MLA_Q_LORA:MLA_Q_LORA + MLA_KV_LORA]) * kvng_ref[...]).astype(BF16)
    k_pe = lat[:, MLA_Q_LORA + MLA_KV_LORA:]
    qf = _dot(q_lat, wuq_ref[...])
    kf = _dot(kv_lat, wk_ref[...])
    v_ref[...] = (_dot(kv_lat, wv_ref[...]) + vone_ref[...]).astype(BF16)

    ang = freq_ref[...] * pos_ref[...]
    cos_f, sin_f = jnp.cos(ang), jnp.sin(ang)
    rest = hl // 2 - half
    cos_t = jnp.concatenate([cos_f, jnp.ones((rest, tm), F32)] * 2, axis=0)
    sin_t = jnp.concatenate([-sin_f, jnp.zeros((rest, tm), F32), sin_f, jnp.zeros((rest, tm), F32)], axis=0)

    def token_major(tab):
        return jnp.concatenate([tab[:, r:r + LANES].T for r in range(0, tm, LANES)], axis=0)

    cos = jnp.concatenate([token_major(cos_t)] * 2, axis=1)
    sin = jnp.concatenate([token_major(sin_t)] * 2, axis=1)
    q_scale = MLA_QK ** -0.5 * LOG2_E
    blk_r = lax.broadcasted_iota(jnp.int32, (2 * hl, 2 * hl), 0) // hl
    blk_c = lax.broadcasted_iota(jnp.int32, (2 * hl, 2 * hl), 1) // hl
    head_ones = jnp.where(blk_r == blk_c, 1.0, 0.0).astype(BF16)

    def pair_norm_rope(t, gain):
        sq = _dot((t * t).astype(BF16), head_ones)
        t = t * lax.rsqrt(sq + MLA_QK * NORM_EPS) * gain
        partner = jnp.concatenate([pltpu.roll(t[:, :hl], hl // 2, 1), pltpu.roll(t[:, hl:], hl // 2, 1)], axis=1)
        return t * cos + partner * sin

    q_gain = jnp.concatenate([qhg_ref[...]] * 2, axis=1) * (MLA_QK ** 0.5 * q_scale)
    k_gain = jnp.concatenate([khg_ref[...]] * 2, axis=1) * (MLA_QK ** 0.5)
    k_pe2 = jnp.concatenate([k_pe] * 2, axis=1)
    for pr in range(MLA_HEADS // 2):
        sl = slice(2 * pr * hl, 2 * (pr + 1) * hl)
        q_ref[:, sl] = pair_norm_rope(qf[:, sl], q_gain).astype(BF16)
        k_ref[:, sl] = pair_norm_rope(kf[:, sl] + k_pe2, k_gain).astype(BF16)


def _mla_in_call(x, pos, mod, g, win, qng, kvng, wuq, wk, wv, qhg, khg, freq, vone, *, seq):
    t, d = x.shape
    tm = ROW_TILE
    x_spec, mod_spec = _row_specs(t, d, seq, tm)
    qk_w = MLA_HEADS * MLA_HEAD_LANES
    v_w = qk_w
    return pl.pallas_call(
        _mla_in_kernel,
        grid=(t // tm,),
        in_specs=[x_spec, pl.BlockSpec((None, 1, tm), lambda i: (i, 0, 0)), mod_spec, _resident((1, d)),
                  _resident(win.shape), _resident(qng.shape), _resident(kvng.shape),
                  _resident(wuq.shape), _resident(wk.shape), _resident(wv.shape),
                  _resident(qhg.shape), _resident(khg.shape), _resident(freq.shape),
                  _resident(vone.shape)],
        out_specs=[pl.BlockSpec((tm, qk_w), lambda i: (i, 0)),
                   pl.BlockSpec((tm, qk_w), lambda i: (i, 0)),
                   pl.BlockSpec((tm, v_w), lambda i: (i, 0))],
        out_shape=[jax.ShapeDtypeStruct((t, qk_w), BF16),
                   jax.ShapeDtypeStruct((t, qk_w), BF16),
                   jax.ShapeDtypeStruct((t, v_w), BF16)],
        compiler_params=_params("arbitrary"),
        name="mla_in",
    )(x, pos.reshape(t // tm, 1, tm), mod, g, win, qng, kvng, wuq, wk, wv, qhg, khg, freq, vone)


def _attn_kernel(q_ref, k_ref, v_ref, o_ref):
    hl = MLA_HEAD_LANES
    rows = q_ref.shape[0] // ATTN_ROW_SPLIT
    lane = lax.broadcasted_iota(jnp.int32, (rows, o_ref.shape[1]), 1)
    items = [(part, hd) for part in range(ATTN_ROW_SPLIT) for hd in range(2)]
    s = [_dot_nt(q_ref[part * rows:(part + 1) * rows, hd * hl:(hd + 1) * hl],
                 k_ref[:, hd * hl:(hd + 1) * hl]) for part, hd in items]
    outs = []
    for idx, (part, hd) in enumerate(items):
        p = jnp.exp2(s[idx] - jnp.max(s[idx], axis=-1, keepdims=True))
        r = _dot(p.astype(BF16), v_ref[:, hd * hl:(hd + 1) * hl])
        outs.append(r / r[:, MLA_V:MLA_V + 1])
    for part in range(ATTN_ROW_SPLIT):
        second = pltpu.roll(outs[2 * part + 1], MLA_V, 1)
        out = jnp.where(lane >= MLA_V, second, outs[2 * part])
        o_ref[part * rows:(part + 1) * rows, :] = out.astype(BF16)


def _attn_call(q, k, v, *, batch, seq):
    tq = ATTN_Q_TILE
    nq = seq // tq
    pair_qk = 2 * MLA_HEAD_LANES
    pair_v = 2 * MLA_V
    return pl.pallas_call(
        _attn_kernel,
        grid=(batch, MLA_HEADS // 2, nq),
        in_specs=[pl.BlockSpec((tq, pair_qk), lambda b, p, i: (b * nq + i, p)),
                  pl.BlockSpec((seq, pair_qk), lambda b, p, i: (b, p)),
                  pl.BlockSpec((seq, pair_qk), lambda b, p, i: (b, p))],
        out_specs=pl.BlockSpec((tq, pair_v), lambda b, p, i: (b * nq + i, p)),
        out_shape=jax.ShapeDtypeStruct((q.shape[0], MLA_HEADS * MLA_V), BF16),
        compiler_params=_params("arbitrary", "arbitrary", "arbitrary"),
        name="attn",
    )(q, k, v)


def _mla_head_perm():
    half = MLA_ROPE // 2
    perm = [-1] * MLA_HEAD_LANES
    for l in range(half):
        perm[l] = MLA_NOPE + l
        perm[MLA_HEAD_LANES // 2 + l] = MLA_NOPE + half + l
    nope_lo = MLA_HEAD_LANES // 2 - half
    for l in range(nope_lo):
        perm[half + l] = l
    for l in range(MLA_NOPE - nope_lo):
        perm[MLA_HEAD_LANES // 2 + half + l] = nope_lo + l
    return perm


def _gather_cols(w, perm):
    idx = jnp.array([max(p, 0) for p in perm], jnp.int32)
    keep = jnp.array([p >= 0 for p in perm])
    return jnp.where(keep, jnp.take(w, idx, axis=-1), 0.0)


def _mla_layout(w_in, w_uq, w_ukv, q_head_g, k_head_g):
    perm = _mla_head_perm()
    lat = MLA_Q_LORA + MLA_KV_LORA
    pe_perm = [p - MLA_NOPE if p >= MLA_NOPE else -1 for p in perm]
    win = jnp.concatenate([w_in[:, :lat], _gather_cols(w_in[:, lat:], pe_perm)], axis=1)
    wuq = _gather_cols(w_uq.reshape(MLA_Q_LORA, MLA_HEADS, MLA_QK), perm)
    wuq = wuq.reshape(MLA_Q_LORA, MLA_HEADS * MLA_HEAD_LANES)
    kv = w_ukv.reshape(MLA_KV_LORA, MLA_HEADS, MLA_NOPE + MLA_V)
    nope_perm = [p if 0 <= p < MLA_NOPE else -1 for p in perm]
    wk = _gather_cols(kv[:, :, :MLA_NOPE], nope_perm).reshape(MLA_KV_LORA, MLA_HEADS * MLA_HEAD_LANES)
    v_pad = [(0, 0), (0, 0), (0, MLA_HEAD_LANES - MLA_V)]
    wv = jnp.pad(kv[:, :, MLA_NOPE:], v_pad).reshape(MLA_KV_LORA, MLA_HEADS * MLA_HEAD_LANES)
    vone = jnp.tile((jnp.arange(MLA_HEAD_LANES) == MLA_V).astype(F32), MLA_HEADS)[None, :]
    qhg = _gather_cols(q_head_g[None, :], perm)
    khg = _gather_cols(k_head_g[None, :], perm)
    freq = (ROPE_THETA ** (-jnp.arange(0, MLA_ROPE, 2, dtype=F32) / MLA_ROPE))[:, None]
    return (win.astype(BF16), wuq.astype(BF16), wk.astype(BF16), wv.astype(BF16), qhg, khg, freq, vone)


def _ssd_layout(w_in, conv_w, conv_b, dt_bias, a_log, d_skip):
    d_inner = SSD_GROUPS * GROUP_LANES
    bc_ch = SSD_GROUPS * SSD_STATE
    conv_ch = d_inner + 2 * bc_ch
    heads = SSD_GROUPS * SSD_HEADS_PER_GROUP
    wz = w_in[:, :d_inner].astype(BF16)
    wxbc = w_in[:, d_inner:d_inner + conv_ch].astype(BF16)

    def conv_by_group(v):
        rows = v.shape[0]
        parts = [v[:, :d_inner].reshape(rows, SSD_GROUPS, GROUP_LANES),
                 v[:, d_inner:d_inner + bc_ch].reshape(rows, SSD_GROUPS, SSD_STATE),
                 v[:, d_inner + bc_ch:].reshape(rows, SSD_GROUPS, SSD_STATE)]
        return jnp.swapaxes(jnp.concatenate(parts, axis=-1), 0, 1)

    def by_group(v):
        lead = v.shape[:-1]
        v = v.reshape(lead + (2, SSD_GROUPS, SSD_HEADS_PER_GROUP))
        v = jnp.swapaxes(v, -3, -2).reshape(lead + (2 * heads,))
        return jnp.pad(v, [(0, 0)] * len(lead) + [(0, LANES - 2 * heads)])

    wdt = by_group(w_in[:, d_inner + conv_ch:]).astype(BF16)
    dtb = by_group(dt_bias.reshape(1, 2 * heads))
    a_log_cols = by_group(a_log.reshape(1, 2 * heads))
    d_skip_col = jnp.broadcast_to(jnp.repeat(d_skip, SSD_HEAD_DIM)[:, None], (d_inner, LANES))
    return wz, wxbc, wdt, dtb, a_log_cols, d_skip_col, conv_by_group(conv_w), conv_by_group(conv_b[None, :])


def kernel(x, c, positions, norm_g, w_mod, b_mod, ffn_w_gate, ffn_w_up, ffn_w_down, ssd_w_in, ssd_conv_w,
           ssd_conv_b, ssd_dt_bias, ssd_a_log, ssd_d, ssd_norm_g, ssd_w_out, mla_w_in, mla_q_norm_g,
           mla_kv_norm_g, mla_w_uq, mla_w_ukv, mla_q_head_g, mla_k_head_g, mla_w_out):
    batch, seq, d = x.shape
    depth = w_mod.shape[0]
    t = batch * seq
    n_mixers = 2
    xf = x.reshape(t, d)
    pos = positions.astype(F32).reshape(t, 1)
    mod = _mod_call(c, w_mod, b_mod).reshape(depth, batch, 3 * N_SUBLAYERS, d)
    wg, wu, wd = (w.astype(BF16) for w in (ffn_w_gate, ffn_w_up, ffn_w_down))

    for i in range(depth):
        g = norm_g[i][:, None, :]
        xf = _ffn_call(xf, mod[i], g[0], wg[i, 0], wu[i, 0], wd[i, 0], sub=0, seq=seq)
        j = i // n_mixers
        if i % n_mixers == 0:
            wz, wxbc, wdt, dtb, a_log_cols, d_skip_col, conv_w_g, conv_b_g = _ssd_layout(
                ssd_w_in[j], ssd_conv_w[j], ssd_conv_b[j], ssd_dt_bias[j], ssd_a_log[j], ssd_d[j])
            z, xbc, dt_raw = _ssd_in_call(xf, mod[i], g[1], wz, wxbc, wdt, seq=seq)
            cum, dtt, cumt = _ssd_dt_call(dt_raw, dtb, a_log_cols, batch=batch, seq=seq)
            y = _ssd_core_call(z, xbc, cum, dtt, cumt, conv_w_g, conv_b_g, d_skip_col,
                               ssd_norm_g[j][None, :], batch=batch, seq=seq)
            w_out = ssd_w_out[j].astype(BF16)
        else:
            win, wuq, wk, wv, qhg, khg, freq, vone = _mla_layout(
                mla_w_in[j], mla_w_uq[j], mla_w_ukv[j], mla_q_head_g[j], mla_k_head_g[j])
            q, k, v = _mla_in_call(xf, pos, mod[i], g[1], win, mla_q_norm_g[j][None, :],
                                   mla_kv_norm_g[j][None, :], wuq, wk, wv, qhg, khg, freq, vone, seq=seq)
            y = _attn_call(q, k, v, batch=batch, seq=seq)
            w_out = mla_w_out[j].astype(BF16)
        xf = _proj_ffn_call(xf, y, w_out, mod[i], g[2], wg[i, 1], wu[i, 1], wd[i, 1], seq=seq)
    return xf.reshape(batch, seq, d)
```

```python
import functools

import jax
import jax.numpy as jnp
from jax import lax
from jax.experimental import pallas as pl
from jax.experimental.pallas import tpu as pltpu

F32 = jnp.float32
BF16 = jnp.bfloat16

N_SUBLAYERS = 3
FFN_RES_WEIGHT = 0.5
NORM_EPS = 1e-6
SSD_HEAD_DIM = 64
SSD_GROUPS = 8
SSD_HEADS_PER_GROUP = 4
SSD_STATE = 128
SSD_CONV = 5
SSD_CHUNK = 128
MLA_HEADS = 16
MLA_Q_LORA = 384
MLA_KV_LORA = 256
MLA_NOPE = 64
MLA_ROPE = 32
MLA_V = 64
MLA_QK = MLA_NOPE + MLA_ROPE
ROPE_THETA = 10000.0
LOG2_E = 1.4426950408889634

LANES = 128
BF16_SUBLANES = 16
VMEM_LIMIT_BYTES = 56 * 1024 * 1024

ROW_TILE = 512
MOD_COL_BLOCKS = 8
ATTN_Q_TILE = 1024
ATTN_ROW_SPLIT = 2
MLA_HEAD_LANES = LANES
GROUP_LANES = SSD_HEADS_PER_GROUP * SSD_HEAD_DIM
SSD_UNROLL = 16


def _params(*semantics):
    return pltpu.CompilerParams(dimension_semantics=semantics,
                                vmem_limit_bytes=VMEM_LIMIT_BYTES)


def _silu(v):
    return v * jax.nn.sigmoid(v)


def _rms(v):
    ms = jnp.mean(v * v, axis=-1, keepdims=True)
    return v * lax.rsqrt(ms + NORM_EPS)


def _modulated(x, g, mod_ref, sub):
    shift = mod_ref[3 * sub:3 * sub + 1, :]
    scale = mod_ref[3 * sub + 1:3 * sub + 2, :]
    return _rms(x) * g * (1.0 + scale) + shift


def _dot(a, b):
    return jnp.dot(a, b, preferred_element_type=F32)


def _dot_nt(a, b):
    return lax.dot_general(a, b, (((1,), (1,)), ((), ())), preferred_element_type=F32)


def _mod_kernel(c_ref, w_ref, b_ref, o_ref):
    o_ref[...] = _dot(_silu(c_ref[...]), w_ref[...]) + b_ref[...]


def _mod_call(c, w_mod, b_mod):
    depth, d, n = w_mod.shape
    bsz = c.shape[0]
    tn = n // MOD_COL_BLOCKS
    return pl.pallas_call(
        _mod_kernel,
        grid=(depth, n // tn),
        in_specs=[
            pl.BlockSpec((bsz, d), lambda i, j: (0, 0)),
            pl.BlockSpec((None, d, tn), lambda i, j: (i, 0, j)),
            pl.BlockSpec((None, 1, tn), lambda i, j: (i, 0, j)),
        ],
        out_specs=pl.BlockSpec((None, bsz, tn), lambda i, j: (i, 0, j)),
        out_shape=jax.ShapeDtypeStruct((depth, bsz, n), F32),
        compiler_params=_params("arbitrary", "arbitrary"),
        name="mod",
    )(c, w_mod, b_mod.reshape(depth, 1, n))


def _ff_chunks(d_ff):
    half = (d_ff // 512) * 256
    return ((0, half), (half, d_ff)) if 0 < half < d_ff else ((0, d_ff),)


def _ffn_apply(x, mod_ref, g_ref, wg_ref, wu_ref, wd_ref, sub):
    h = _modulated(x, g_ref[...], mod_ref, sub).astype(BF16)
    acc = None
    for lo, hi in _ff_chunks(wg_ref.shape[1]):
        gate = _dot(h, wg_ref[:, lo:hi])
        up = _dot(h, wu_ref[:, lo:hi])
        part = _dot((_silu(gate) * up).astype(BF16), wd_ref[lo:hi, :])
        acc = part if acc is None else acc + part
    res_gate = mod_ref[3 * sub + 2:3 * sub + 3, :]
    return x + (FFN_RES_WEIGHT * res_gate) * acc


def _ffn_kernel(x_ref, mod_ref, g_ref, wg_ref, wu_ref, wd_ref, o_ref, *, sub):
    o_ref[...] = _ffn_apply(x_ref[...], mod_ref, g_ref, wg_ref, wu_ref, wd_ref, sub)


def _proj_ffn_kernel(x_ref, y_ref, wo_ref, mod_ref, g_ref, wg_ref, wu_ref, wd_ref, o_ref):
    mix_gate = mod_ref[3 * 1 + 2:3 * 1 + 3, :]
    x = x_ref[...] + mix_gate * _dot(y_ref[...], wo_ref[...])
    o_ref[...] = _ffn_apply(x, mod_ref, g_ref, wg_ref, wu_ref, wd_ref, 2)


def _resident(shape):
    return pl.BlockSpec(shape, lambda *_: (0,) * len(shape), pipeline_mode=pl.Buffered(1))


def _row_specs(t, d, seq, tm):
    tiles_per_seq = seq // tm
    x_spec = pl.BlockSpec((tm, d), lambda i: (i, 0))
    mod_spec = pl.BlockSpec((None, 3 * N_SUBLAYERS, d), lambda i: (i // tiles_per_seq, 0, 0))
    return x_spec, mod_spec


def _ffn_call(x, mod, g, wg, wu, wd, *, sub, seq):
    t, d = x.shape
    tm = ROW_TILE
    x_spec, mod_spec = _row_specs(t, d, seq, tm)
    return pl.pallas_call(
        functools.partial(_ffn_kernel, sub=sub),
        grid=(t // tm,),
        in_specs=[x_spec, mod_spec, _resident((1, d)), _resident(wg.shape), _resident(wu.shape),
                  _resident(wd.shape)],
        out_specs=x_spec,
        out_shape=jax.ShapeDtypeStruct((t, d), F32),
        compiler_params=_params("arbitrary"),
        name="ffn",
    )(x, mod, g, wg, wu, wd)


def _proj_ffn_call(x, y, wo, mod, g, wg, wu, wd, *, seq):
    t, d = x.shape
    tm = ROW_TILE
    x_spec, mod_spec = _row_specs(t, d, seq, tm)
    y_spec = pl.BlockSpec((tm, y.shape[1]), lambda i: (i, 0))
    return pl.pallas_call(
        _proj_ffn_kernel,
        grid=(t // tm,),
        in_specs=[x_spec, y_spec, _resident(wo.shape), mod_spec, _resident((1, d)), _resident(wg.shape),
                  _resident(wu.shape), _resident(wd.shape)],
        out_specs=x_spec,
        out_shape=jax.ShapeDtypeStruct((t, d), F32),
        compiler_params=_params("arbitrary"),
        name="proj_ffn",
    )(x, y, wo, mod, g, wg, wu, wd)


def _ssd_in_kernel(x_ref, mod_ref, g_ref, wz_ref, wxbc_ref, wdt_ref, z_ref, xbc_ref, dt_ref):
    h = _modulated(x_ref[...], g_ref[...], mod_ref, 1).astype(BF16)
    z_ref[...] = _dot(h, wz_ref[...]).astype(BF16)
    xbc_ref[...] = _dot(h, wxbc_ref[...]).astype(BF16)
    dt_ref[...] = _dot(h, wdt_ref[...])


def _ssd_in_call(x, mod, g, wz, wxbc, wdt, *, seq):
    t, d = x.shape
    tm = ROW_TILE
    x_spec, mod_spec = _row_specs(t, d, seq, tm)
    widths = (wz.shape[1], wxbc.shape[1], wdt.shape[1])
    return pl.pallas_call(
        _ssd_in_kernel,
        grid=(t // tm,),
        in_specs=[x_spec, mod_spec, _resident((1, d)), _resident(wz.shape), _resident(wxbc.shape),
                  _resident(wdt.shape)],
        out_specs=[pl.BlockSpec((tm, w), lambda i: (i, 0)) for w in widths],
        out_shape=[jax.ShapeDtypeStruct((t, widths[0]), BF16),
                   jax.ShapeDtypeStruct((t, widths[1]), BF16),
                   jax.ShapeDtypeStruct((t, widths[2]), F32)],
        compiler_params=_params("arbitrary"),
        name="ssd_in",
    )(x, mod, g, wz, wxbc, wdt)


def _split3(v):
    v1 = v.astype(BF16)
    rem = v - v1.astype(F32)
    v2 = rem.astype(BF16)
    return v1, v2, (rem - v2.astype(F32)).astype(BF16)


def _ssd_dt_kernel(dtr_ref, dtb_ref, alog_ref, cum_ref, dtt_ref, cumt_ref):
    seq = dtr_ref.shape[0]
    q = SSD_CHUNK
    row_i = lax.broadcasted_iota(jnp.int32, (q, q), 0)
    col_j = lax.broadcasted_iota(jnp.int32, (q, q), 1)
    tri = jnp.where(row_i >= col_j, 1.0, 0.0).astype(BF16)
    lane = lax.broadcasted_iota(jnp.int32, (q, LANES), 1)
    is_fwd = (lane // SSD_HEADS_PER_GROUP) % 2 == 0
    a_coef = -jnp.exp(alog_ref[...]) * LOG2_E

    def chunk(c, carry):
        r0 = pl.multiple_of(c * q, q)
        dt = jax.nn.softplus(dtr_ref[pl.ds(r0, q), :] + dtb_ref[...])
        a = dt * a_coef
        a1, a2, a3 = _split3(a)
        prefix = _dot(tri, a1) + _dot(tri, a2) + _dot(tri, a3)
        cum = jnp.where(is_fwd, prefix, prefix[q - 1:q, :] - prefix + a)
        cum_ref[pl.ds(r0, q), :] = cum
        dtt_ref[:, pl.ds(r0, q)] = dt.T
        cumt_ref[:, pl.ds(r0, q)] = cum.T
        return carry

    lax.fori_loop(0, seq // q, chunk, 0, unroll=SSD_UNROLL)


def _ssd_dt_call(dt_raw, dt_bias, a_log, *, batch, seq):
    t = dt_raw.shape[0]
    head_major = pl.BlockSpec((None, LANES, seq), lambda b: (b, 0, 0))
    return pl.pallas_call(
        _ssd_dt_kernel,
        grid=(batch,),
        in_specs=[pl.BlockSpec((seq, LANES), lambda b: (b, 0)),
                  pl.BlockSpec((1, LANES), lambda b: (0, 0)),
                  pl.BlockSpec((1, LANES), lambda b: (0, 0))],
        out_specs=[pl.BlockSpec((seq, LANES), lambda b: (b, 0)), head_major, head_major],
        out_shape=[jax.ShapeDtypeStruct((t, LANES), F32),
                   jax.ShapeDtypeStruct((batch, LANES, seq), F32),
                   jax.ShapeDtypeStruct((batch, LANES, seq), F32)],
        compiler_params=_params("arbitrary"),
        name="ssd_dt",
    )(dt_raw, dt_bias, a_log)


def _ssd_core_kernel(xr_ref, br_ref, cr_ref, z_ref, cum_ref, dtt_ref, cumt_ref, cw_ref, cb_ref,
                     dsk_ref, ng_ref, o_ref, xt_ref, cs_ref, yt_ref, st_ref, dec_ref):
    seq = xr_ref.shape[0]
    q = SSD_CHUNK
    halo = BF16_SUBLANES
    nc = seq // q
    gl = GROUP_LANES
    n = SSD_STATE
    hp = SSD_HEAD_DIM
    heads = SSD_HEADS_PER_GROUP
    centre = SSD_CONV // 2
    group = pl.program_id(1)

    col_shift = jnp.where(group == 0, 0, LANES - 2 * heads * group)
    src_j = lax.broadcasted_iota(jnp.int32, (q, q), 0)
    dst_i = lax.broadcasted_iota(jnp.int32, (q, q), 1)
    causal = (dst_i >= src_j, dst_i <= src_j)
    def shift_block(k):
        rows = q if k == centre else q + halo
        first = (halo if k < centre else 0) + k - centre
        out_row = lax.broadcasted_iota(jnp.int32, (q, rows), 0)
        win_row = lax.broadcasted_iota(jnp.int32, (q, rows), 1)
        return jnp.where(win_row == out_row + first, 1.0, 0.0).astype(BF16)

    shift_all = jnp.concatenate([shift_block(k) for k in range(SSD_CONV)], axis=1)
    cw_bf = cw_ref[...].astype(BF16)

    def lanes_of(r0, rows):
        return jnp.concatenate([xr_ref[pl.ds(r0, rows), :], br_ref[pl.ds(r0, rows), :],
                                cr_ref[pl.ds(r0, rows), :]], axis=1)

    def head_rows(vals, direction):
        return jnp.concatenate(
            [jnp.broadcast_to(vals[heads * direction + h:heads * direction + h + 1, :], (hp, vals.shape[1]))
             for h in range(heads)], axis=0)

    def conv_part(c):
        r0 = pl.multiple_of(c * q, q)
        rp = pl.multiple_of(jnp.maximum(r0 - halo, 0), halo)
        rn = pl.multiple_of(jnp.minimum(r0 + q, seq - halo), halo)
        cur = lanes_of(r0, q)
        prev = lanes_of(rp, halo)
        nxt = lanes_of(rn, halo)
        prev = jnp.where(c == 0, jnp.zeros_like(prev), prev)
        nxt = jnp.where(c == nc - 1, jnp.zeros_like(nxt), nxt)
        weighted = []
        for k in range(SSD_CONV):
            w_k = cw_bf[k:k + 1, :]
            if k < centre:
                weighted += [prev * w_k, cur * w_k]
            elif k > centre:
                weighted += [cur * w_k, nxt * w_k]
            else:
                weighted.append(cur * w_k)
        act = _silu(cb_ref[...] + _dot(shift_all, jnp.concatenate(weighted, axis=0)))
        x = act[:, :gl]
        xt = jnp.concatenate([x[:, :LANES].T, x[:, LANES:].T], axis=0)
        b_bf = act[:, gl:gl + n].astype(BF16)
        c_bf = act[:, gl + n:].astype(BF16)
        xt_ref[c] = xt
        cs_ref[pl.ds(r0, q), :] = c_bf
        return xt, b_bf, c_bf

    def local_part(c, xt, b_bf, cbt):
        r0 = pl.multiple_of(c * q, q)
        cum_c = pltpu.roll(cum_ref[pl.ds(r0, q), :], col_shift, 1)
        dt_r = dtt_ref[:, pl.ds(r0, q)]
        cum_r = cumt_ref[:, pl.ds(r0, q)]
        total = jnp.concatenate([cum_r[:heads, q - 1:q], cum_r[heads:, 0:1]], axis=0)
        to_edge = jnp.exp2(total - cum_r)
        dec_ref[c] = jnp.broadcast_to(jnp.exp2(total), (2 * heads, n))
        xdt = [xt[(k % heads) * hp:(k % heads + 1) * hp, :] * dt_r[k:k + 1, :] for k in range(2 * heads)]
        st_ref[c] = _dot(jnp.concatenate([(xdt[k] * to_edge[k:k + 1, :]).astype(BF16)
                                          for k in range(2 * heads)], axis=0), b_bf)
        y_heads = []
        for h in range(heads):
            ks = (h, heads + h)
            lhs = jnp.concatenate([xdt[k].astype(BF16) for k in ks], axis=1)
            rhs = jnp.concatenate(
                [(cbt * jnp.exp2(jnp.where(causal[k // heads], cum_r[k:k + 1, :] - cum_c[:, k:k + 1],
                                           -jnp.inf))).astype(BF16) for k in ks], axis=0)
            y_heads.append(_dot(lhs, rhs))
        yt_ref[c] = jnp.concatenate(y_heads, axis=0)

    def phase_a(c, conv_c):
        xt, b_bf, c_bf = conv_c
        cbt = _dot_nt(b_bf, c_bf)
        conv_next = conv_part(jnp.minimum(c + 1, nc - 1))
        local_part(c, xt, b_bf, cbt)
        return conv_next

    lax.fori_loop(0, nc, phase_a, conv_part(0), unroll=SSD_UNROLL)

    def step(t, carry):
        new = []
        for d in range(2):
            c = t if d == 0 else nc - 1 - t
            own = st_ref[c, d * gl:(d + 1) * gl]
            st_ref[c, d * gl:(d + 1) * gl] = carry[d]
            new.append(carry[d] * head_rows(dec_ref[c], d) + own)
        return tuple(new)

    lax.fori_loop(0, nc, step, (jnp.zeros((gl, n), F32), jnp.zeros((gl, n), F32)))

    def phase_c(c, carry):
        r0 = pl.multiple_of(c * q, q)
        grow = jnp.exp2(cumt_ref[:, pl.ds(r0, q)])
        c_bf = cs_ref[pl.ds(r0, q), :]
        off = _dot_nt(st_ref[c].astype(BF16), c_bf)
        yt = yt_ref[c] + xt_ref[c] * dsk_ref[...]
        for d in range(2):
            yt = yt + off[d * gl:(d + 1) * gl] * head_rows(grow, d)
        y = jnp.concatenate([yt[:LANES].T, yt[LANES:].T], axis=1)
        y = y * _silu(z_ref[pl.ds(r0, q), :].astype(F32))
        o_ref[pl.ds(r0, q), :] = (_rms(y) * ng_ref[...]).astype(BF16)
        return carry

    lax.fori_loop(0, nc, phase_c, 0, unroll=SSD_UNROLL)


def _ssd_core_call(z, xbc, cum, dtt, cumt, conv_w_g, conv_b_g, d_skip_col, norm_g, *, batch, seq):
    t, d_inner = z.shape
    n = SSD_STATE
    gl = GROUP_LANES
    x_blocks = d_inner // n
    c_first = x_blocks + SSD_GROUPS
    nc = seq // SSD_CHUNK
    dir_heads = 2 * SSD_HEADS_PER_GROUP
    conv_lanes = conv_w_g.shape[-1]
    in_specs = [
        pl.BlockSpec((seq, gl), lambda b, g: (b, g)),
        pl.BlockSpec((seq, n), lambda b, g: (b, x_blocks + g)),
        pl.BlockSpec((seq, n), lambda b, g: (b, c_first + g)),
        pl.BlockSpec((seq, gl), lambda b, g: (b, g)),
        pl.BlockSpec((seq, LANES), lambda b, g: (b, 0)),
        pl.BlockSpec((None, dir_heads, seq), lambda b, g: (b, g, 0)),
        pl.BlockSpec((None, dir_heads, seq), lambda b, g: (b, g, 0)),
        pl.BlockSpec((None, SSD_CONV, conv_lanes), lambda b, g: (g, 0, 0)),
        pl.BlockSpec((None, 1, conv_lanes), lambda b, g: (g, 0, 0)),
        pl.BlockSpec((gl, LANES), lambda b, g: (g, 0)),
        pl.BlockSpec((1, gl), lambda b, g: (0, g)),
    ]
    return pl.pallas_call(
        _ssd_core_kernel,
        grid=(batch, SSD_GROUPS),
        in_specs=in_specs,
        out_specs=pl.BlockSpec((seq, gl), lambda b, g: (b, g)),
        out_shape=jax.ShapeDtypeStruct((t, d_inner), BF16),
        scratch_shapes=[
            pltpu.VMEM((nc, gl, SSD_CHUNK), F32),
            pltpu.VMEM((seq, n), BF16),
            pltpu.VMEM((nc, gl, SSD_CHUNK), F32),
            pltpu.VMEM((nc, 2 * gl, n), F32),
            pltpu.VMEM((nc, dir_heads, n), F32),
        ],
        compiler_params=_params("arbitrary", "arbitrary"),
        name="ssd_core",
    )(xbc, xbc, xbc, z, cum, dtt, cumt, conv_w_g, conv_b_g, d_skip_col, norm_g)


def _mla_in_kernel(x_ref, pos_ref, mod_ref, g_ref, win_ref, qng_ref, kvng_ref, wuq_ref, wk_ref, wv_ref,
                   qhg_ref, khg_ref, freq_ref, vone_ref, q_ref, k_ref, v_ref):
    hl = MLA_HEAD_LANES
    tm = x_ref.shape[0]
    half = MLA_ROPE // 2
    h = _modulated(x_ref[...], g_ref[...], mod_ref, 1).astype(BF16)
    lat = _dot(h, win_ref[...])
    q_lat = (_rms(lat[:, :MLA_Q_LORA]) * qng_ref[...]).astype(BF16)
    kv_lat = (_rms(lat[:, MLA_Q_LORA:MLA_Q_LORA + MLA_KV_LORA]) * kvng_ref[...]).astype(BF16)
    k_pe = lat[:, MLA_Q_LORA + MLA_KV_LORA:]
    qf = _dot(q_lat, wuq_ref[...])
    kf = _dot(kv_lat, wk_ref[...])
    v_ref[...] = (_dot(kv_lat, wv_ref[...]) + vone_ref[...]).astype(BF16)

    ang = freq_ref[...] * pos_ref[...]
    cos_f, sin_f = jnp.cos(ang), jnp.sin(ang)
    rest = hl // 2 - half
    cos_t = jnp.concatenate([cos_f, jnp.ones((rest, tm), F32)] * 2, axis=0)
    sin_t = jnp.concatenate([-sin_f, jnp.zeros((rest, tm), F32), sin_f, jnp.zeros((rest, tm), F32)], axis=0)

    def token_major(tab):
        return jnp.concatenate([tab[:, r:r + LANES].T for r in range(0, tm, LANES)], axis=0)

    cos = jnp.concatenate([token_major(cos_t)] * 2, axis=1)
    sin = jnp.concatenate([token_major(sin_t)] * 2, axis=1)
    q_scale = MLA_QK ** -0.5 * LOG2_E
    blk_r = lax.broadcasted_iota(jnp.int32, (2 * hl, 2 * hl), 0) // hl
    blk_c = lax.broadcasted_iota(jnp.int32, (2 * hl, 2 * hl), 1) // hl
    head_ones = jnp.where(blk_r == blk_c, 1.0, 0.0).astype(BF16)

    def pair_norm_rope(t, gain):
        sq = _dot((t * t).astype(BF16), head_ones)
        t = t * lax.rsqrt(sq + MLA_QK * NORM_EPS) * gain
        partner = jnp.concatenate([pltpu.roll(t[:, :hl], hl // 2, 1), pltpu.roll(t[:, hl:], hl // 2, 1)], axis=1)
        return t * cos + partner * sin

    q_gain = jnp.concatenate([qhg_ref[...]] * 2, axis=1) * (MLA_QK ** 0.5 * q_scale)
    k_gain = jnp.concatenate([khg_ref[...]] * 2, axis=1) * (MLA_QK ** 0.5)
    k_pe2 = jnp.concatenate([k_pe] * 2, axis=1)
    for pr in range(MLA_HEADS // 2):
        sl = slice(2 * pr * hl, 2 * (pr + 1) * hl)
        q_ref[:, sl] = pair_norm_rope(qf[:, sl], q_gain).astype(BF16)
        k_ref[:, sl] = pair_norm_rope(kf[:, sl] + k_pe2, k_gain).astype(BF16)


def _mla_in_call(x, pos, mod, g, win, qng, kvng, wuq, wk, wv, qhg, khg, freq, vone, *, seq):
    t, d = x.shape
    tm = ROW_TILE
    x_spec, mod_spec = _row_specs(t, d, seq, tm)
    qk_w = MLA_HEADS * MLA_HEAD_LANES
    v_w = qk_w
    return pl.pallas_call(
        _mla_in_kernel,
        grid=(t // tm,),
        in_specs=[x_spec, pl.BlockSpec((None, 1, tm), lambda i: (i, 0, 0)), mod_spec, _resident((1, d)),
                  _resident(win.shape), _resident(qng.shape), _resident(kvng.shape),
                  _resident(wuq.shape), _resident(wk.shape), _resident(wv.shape),
                  _resident(qhg.shape), _resident(khg.shape), _resident(freq.shape),
                  _resident(vone.shape)],
        out_specs=[pl.BlockSpec((tm, qk_w), lambda i: (i, 0)),
                   pl.BlockSpec((tm, qk_w), lambda i: (i, 0)),
                   pl.BlockSpec((tm, v_w), lambda i: (i, 0))],
        out_shape=[jax.ShapeDtypeStruct((t, qk_w), BF16),
                   jax.ShapeDtypeStruct((t, qk_w), BF16),
                   jax.ShapeDtypeStruct((t, v_w), BF16)],
        compiler_params=_params("arbitrary"),
        name="mla_in",
    )(x, pos.reshape(t // tm, 1, tm), mod, g, win, qng, kvng, wuq, wk, wv, qhg, khg, freq, vone)


def _attn_kernel(q_ref, k_ref, v_ref, o_ref):
    hl = MLA_HEAD_LANES
    rows = q_ref.shape[0] // ATTN_ROW_SPLIT
    lane = lax.broadcasted_iota(jnp.int32, (rows, o_ref.shape[1]), 1)
    items = [(part, hd) for part in range(ATTN_ROW_SPLIT) for hd in range(2)]
    s = [_dot_nt(q_ref[part * rows:(part + 1) * rows, hd * hl:(hd + 1) * hl],
                 k_ref[:, hd * hl:(hd + 1) * hl]) for part, hd in items]
    outs = []
    for idx, (part, hd) in enumerate(items):
        p = jnp.exp2(s[idx] - jnp.max(s[idx], axis=-1, keepdims=True))
        r = _dot(p.astype(BF16), v_ref[:, hd * hl:(hd + 1) * hl])
        outs.append(r / r[:, MLA_V:MLA_V + 1])
    for part in range(ATTN_ROW_SPLIT):
        second = pltpu.roll(outs[2 * part + 1], MLA_V, 1)
        out = jnp.where(lane >= MLA_V, second, outs[2 * part])
        o_ref[part * rows:(part + 1) * rows, :] = out.astype(BF16)


def _attn_call(q, k, v, *, batch, seq):
    tq = ATTN_Q_TILE
    nq = seq // tq
    pair_qk = 2 * MLA_HEAD_LANES
    pair_v = 2 * MLA_V
    return pl.pallas_call(
        _attn_kernel,
        grid=(batch, MLA_HEADS // 2, nq),
        in_specs=[pl.BlockSpec((tq, pair_qk), lambda b, p, i: (b * nq + i, p)),
                  pl.BlockSpec((seq, pair_qk), lambda b, p, i: (b, p)),
                  pl.BlockSpec((seq, pair_qk), lambda b, p, i: (b, p))],
        out_specs=pl.BlockSpec((tq, pair_v), lambda b, p, i: (b * nq + i, p)),
        out_shape=jax.ShapeDtypeStruct((q.shape[0], MLA_HEADS * MLA_V), BF16),
        compiler_params=_params("arbitrary", "arbitrary", "arbitrary"),
        name="attn",
    )(q, k, v)


def _mla_head_perm():
    half = MLA_ROPE // 2
    perm = [-1] * MLA_HEAD_LANES
    for l in range(half):
        perm[l] = MLA_NOPE + l
        perm[MLA_HEAD_LANES // 2 + l] = MLA_NOPE + half + l
    nope_lo = MLA_HEAD_LANES // 2 - half
    for l in range(nope_lo):
        perm[half + l] = l
    for l in range(MLA_NOPE - nope_lo):
        perm[MLA_HEAD_LANES // 2 + half + l] = nope_lo + l
    return perm


def _gather_cols(w, perm):
    idx = jnp.array([max(p, 0) for p in perm], jnp.int32)
    keep = jnp.array([p >= 0 for p in perm])
    return jnp.where(keep, jnp.take(w, idx, axis=-1), 0.0)


def _mla_layout(w_in, w_uq, w_ukv, q_head_g, k_head_g):
    perm = _mla_head_perm()
    lat = MLA_Q_LORA + MLA_KV_LORA
    pe_perm = [p - MLA_NOPE if p >= MLA_NOPE else -1 for p in perm]
    win = jnp.concatenate([w_in[:, :lat], _gather_cols(w_in[:, lat:], pe_perm)], axis=1)
    wuq = _gather_cols(w_uq.reshape(MLA_Q_LORA, MLA_HEADS, MLA_QK), perm)
    wuq = wuq.reshape(MLA_Q_LORA, MLA_HEADS * MLA_HEAD_LANES)
    kv = w_ukv.reshape(MLA_KV_LORA, MLA_HEADS, MLA_NOPE + MLA_V)
    nope_perm = [p if 0 <= p < MLA_NOPE else -1 for p in perm]
    wk = _gather_cols(kv[:, :, :MLA_NOPE], nope_perm).reshape(MLA_KV_LORA, MLA_HEADS * MLA_HEAD_LANES)
    v_pad = [(0, 0), (0, 0), (0, MLA_HEAD_LANES - MLA_V)]
    wv = jnp.pad(kv[:, :, MLA_NOPE:], v_pad).reshape(MLA_KV_LORA, MLA_HEADS * MLA_HEAD_LANES)
    vone = jnp.tile((jnp.arange(MLA_HEAD_LANES) == MLA_V).astype(F32), MLA_HEADS)[None, :]
    qhg = _gather_cols(q_head_g[None, :], perm)
    khg = _gather_cols(k_head_g[None, :], perm)
    freq = (ROPE_THETA ** (-jnp.arange(0, MLA_ROPE, 2, dtype=F32) / MLA_ROPE))[:, None]
    return (win.astype(BF16), wuq.astype(BF16), wk.astype(BF16), wv.astype(BF16), qhg, khg, freq, vone)


def _ssd_layout(w_in, conv_w, conv_b, dt_bias, a_log, d_skip):
    d_inner = SSD_GROUPS * GROUP_LANES
    bc_ch = SSD_GROUPS * SSD_STATE
    conv_ch = d_inner + 2 * bc_ch
    heads = SSD_GROUPS * SSD_HEADS_PER_GROUP
    wz = w_in[:, :d_inner].astype(BF16)
    wxbc = w_in[:, d_inner:d_inner + conv_ch].astype(BF16)

    def conv_by_group(v):
        rows = v.shape[0]
        parts = [v[:, :d_inner].reshape(rows, SSD_GROUPS, GROUP_LANES),
                 v[:, d_inner:d_inner + bc_ch].reshape(rows, SSD_GROUPS, SSD_STATE),
                 v[:, d_inner + bc_ch:].reshape(rows, SSD_GROUPS, SSD_STATE)]
        return jnp.swapaxes(jnp.concatenate(parts, axis=-1), 0, 1)

    def by_group(v):
        lead = v.shape[:-1]
        v = v.reshape(lead + (2, SSD_GROUPS, SSD_HEADS_PER_GROUP))
        v = jnp.swapaxes(v, -3, -2).reshape(lead + (2 * heads,))
        return jnp.pad(v, [(0, 0)] * len(lead) + [(0, LANES - 2 * heads)])

    wdt = by_group(w_in[:, d_inner + conv_ch:]).astype(BF16)
    dtb = by_group(dt_bias.reshape(1, 2 * heads))
    a_log_cols = by_group(a_log.reshape(1, 2 * heads))
    d_skip_col = jnp.broadcast_to(jnp.repeat(d_skip, SSD_HEAD_DIM)[:, None], (d_inner, LANES))
    return wz, wxbc, wdt, dtb, a_log_cols, d_skip_col, conv_by_group(conv_w), conv_by_group(conv_b[None, :])


def kernel(x, c, positions, norm_g, w_mod, b_mod, ffn_w_gate, ffn_w_up, ffn_w_down, ssd_w_in, ssd_conv_w,
           ssd_conv_b, ssd_dt_bias, ssd_a_log, ssd_d, ssd_norm_g, ssd_w_out, mla_w_in, mla_q_norm_g,
           mla_kv_norm_g, mla_w_uq, mla_w_ukv, mla_q_head_g, mla_k_head_g, mla_w_out):
    batch, seq, d = x.shape
    depth = w_mod.shape[0]
    t = batch * seq
    n_mixers = 2
    xf = x.reshape(t, d)
    pos = positions.astype(F32).reshape(t, 1)
    mod = _mod_call(c, w_mod, b_mod).reshape(depth, batch, 3 * N_SUBLAYERS, d)
    wg, wu, wd = (w.astype(BF16) for w in (ffn_w_gate, ffn_w_up, ffn_w_down))

    for i in range(depth):
        g = norm_g[i][:, None, :]
        xf = _ffn_call(xf, mod[i], g[0], wg[i, 0], wu[i, 0], wd[i, 0], sub=0, seq=seq)
        j = i // n_mixers
        if i % n_mixers == 0:
            wz, wxbc, wdt, dtb, a_log_cols, d_skip_col, conv_w_g, conv_b_g = _ssd_layout(
                ssd_w_in[j], ssd_conv_w[j], ssd_conv_b[j], ssd_dt_bias[j], ssd_a_log[j], ssd_d[j])
            z, xbc, dt_raw = _ssd_in_call(xf, mod[i], g[1], wz, wxbc, wdt, seq=seq)
            cum, dtt, cumt = _ssd_dt_call(dt_raw, dtb, a_log_cols, batch=batch, seq=seq)
            y = _ssd_core_call(z, xbc, cum, dtt, cumt, conv_w_g, conv_b_g, d_skip_col,
                               ssd_norm_g[j][None, :], batch=batch, seq=seq)
            w_out = ssd_w_out[j].astype(BF16)
        else:
            win, wuq, wk, wv, qhg, khg, freq, vone = _mla_layout(
                mla_w_in[j], mla_w_uq[j], mla_w_ukv[j], mla_q_head_g[j], mla_k_head_g[j])
            q, k, v = _mla_in_call(xf, pos, mod[i], g[1], win, mla_q_norm_g[j][None, :],
                                   mla_kv_norm_g[j][None, :], wuq, wk, wv, qhg, khg, freq, vone, seq=seq)
            y = _attn_call(q, k, v, batch=batch, seq=seq)
            w_out = mla_w_out[j].astype(BF16)
        xf = _proj_ffn_call(xf, y, w_out, mod[i], g[2], wg[i, 1], wu[i, 1], wd[i, 1], seq=seq)
    return xf.reshape(batch, seq, d)
```

```python
import functools

import jax
import jax.numpy as jnp
from jax import lax
from jax.experimental import pallas as pl
from jax.experimental.pallas import tpu as pltpu

F32 = jnp.float32
BF16 = jnp.bfloat16

N_SUBLAYERS = 3
FFN_RES_WEIGHT = 0.5
NORM_EPS = 1e-6
SSD_HEAD_DIM = 64
SSD_GROUPS = 8
SSD_HEADS_PER_GROUP = 4
SSD_STATE = 128
SSD_CONV = 5
SSD_CHUNK = 128
MLA_HEADS = 16
MLA_Q_LORA = 384
MLA_KV_LORA = 256
MLA_NOPE = 64
MLA_ROPE = 32
MLA_V = 64
MLA_QK = MLA_NOPE + MLA_ROPE
ROPE_THETA = 10000.0
LOG2_E = 1.4426950408889634

LANES = 128
BF16_SUBLANES = 16
VMEM_LIMIT_BYTES = 56 * 1024 * 1024

ROW_TILE = 512
MOD_COL_BLOCKS = 8
ATTN_Q_TILE = 1024
ATTN_ROW_SPLIT = 2
MLA_HEAD_LANES = LANES
GROUP_LANES = SSD_HEADS_PER_GROUP * SSD_HEAD_DIM
SSD_UNROLL = 16


def _params(*semantics):
    return pltpu.CompilerParams(dimension_semantics=semantics,
                                vmem_limit_bytes=VMEM_LIMIT_BYTES)


def _silu(v):
    return v * jax.nn.sigmoid(v)


def _rms(v):
    ms = jnp.mean(v * v, axis=-1, keepdims=True)
    return v * lax.rsqrt(ms + NORM_EPS)


def _modulated(x, g, mod_ref, sub):
    shift = mod_ref[3 * sub:3 * sub + 1, :]
    scale = mod_ref[3 * sub + 1:3 * sub + 2, :]
    return _rms(x) * g * (1.0 + scale) + shift


def _dot(a, b):
    return jnp.dot(a, b, preferred_element_type=F32)


def _dot_nt(a, b):
    return lax.dot_general(a, b, (((1,), (1,)), ((), ())), preferred_element_type=F32)


def _mod_kernel(c_ref, w_ref, b_ref, o_ref):
    o_ref[...] = _dot(_silu(c_ref[...]), w_ref[...]) + b_ref[...]


def _mod_call(c, w_mod, b_mod):
    depth, d, n = w_mod.shape
    bsz = c.shape[0]
    tn = n // MOD_COL_BLOCKS
    return pl.pallas_call(
        _mod_kernel,
        grid=(depth, n // tn),
        in_specs=[
            pl.BlockSpec((bsz, d), lambda i, j: (0, 0)),
            pl.BlockSpec((None, d, tn), lambda i, j: (i, 0, j)),
            pl.BlockSpec((None, 1, tn), lambda i, j: (i, 0, j)),
        ],
        out_specs=pl.BlockSpec((None, bsz, tn), lambda i, j: (i, 0, j)),
        out_shape=jax.ShapeDtypeStruct((depth, bsz, n), F32),
        compiler_params=_params("arbitrary", "arbitrary"),
        name="mod",
    )(c, w_mod, b_mod.reshape(depth, 1, n))


def _ff_chunks(d_ff):
    half = (d_ff // 512) * 256
    return ((0, half), (half, d_ff)) if 0 < half < d_ff else ((0, d_ff),)


def _ffn_apply(x, mod_ref, g_ref, wg_ref, wu_ref, wd_ref, sub):
    h = _modulated(x, g_ref[...], mod_ref, sub).astype(BF16)
    acc = None
    for lo, hi in _ff_chunks(wg_ref.shape[1]):
        gate = _dot(h, wg_ref[:, lo:hi])
        up = _dot(h, wu_ref[:, lo:hi])
        part = _dot((_silu(gate) * up).astype(BF16), wd_ref[lo:hi, :])
        acc = part if acc is None else acc + part
    res_gate = mod_ref[3 * sub + 2:3 * sub + 3, :]
    return x + (FFN_RES_WEIGHT * res_gate) * acc


def _ffn_kernel(x_ref, mod_ref, g_ref, wg_ref, wu_ref, wd_ref, o_ref, *, sub):
    o_ref[...] = _ffn_apply(x_ref[...], mod_ref, g_ref, wg_ref, wu_ref, wd_ref, sub)


def _proj_ffn_kernel(x_ref, y_ref, wo_ref, mod_ref, g_ref, wg_ref, wu_ref, wd_ref, o_ref):
    mix_gate = mod_ref[3 * 1 + 2:3 * 1 + 3, :]
    x = x_ref[...] + mix_gate * _dot(y_ref[...], wo_ref[...])
    o_ref[...] = _ffn_apply(x, mod_ref, g_ref, wg_ref, wu_ref, wd_ref, 2)


def _resident(shape):
    return pl.BlockSpec(shape, lambda *_: (0,) * len(shape), pipeline_mode=pl.Buffered(1))


def _row_specs(t, d, seq, tm):
    tiles_per_seq = seq // tm
    x_spec = pl.BlockSpec((tm, d), lambda i: (i, 0))
    mod_spec = pl.BlockSpec((None, 3 * N_SUBLAYERS, d), lambda i: (i // tiles_per_seq, 0, 0))
    return x_spec, mod_spec


def _ffn_call(x, mod, g, wg, wu, wd, *, sub, seq):
    t, d = x.shape
    tm = ROW_TILE
    x_spec, mod_spec = _row_specs(t, d, seq, tm)
    return pl.pallas_call(
        functools.partial(_ffn_kernel, sub=sub),
        grid=(t // tm,),
        in_specs=[x_spec, mod_spec, _resident((1, d)), _resident(wg.shape), _resident(wu.shape),
                  _resident(wd.shape)],
        out_specs=x_spec,
        out_shape=jax.ShapeDtypeStruct((t, d), F32),
        compiler_params=_params("arbitrary"),
        name="ffn",
    )(x, mod, g, wg, wu, wd)


def _proj_ffn_call(x, y, wo, mod, g, wg, wu, wd, *, seq):
    t, d = x.shape
    tm = ROW_TILE
    x_spec, mod_spec = _row_specs(t, d, seq, tm)
    y_spec = pl.BlockSpec((tm, y.shape[1]), lambda i: (i, 0))
    return pl.pallas_call(
        _proj_ffn_kernel,
        grid=(t // tm,),
        in_specs=[x_spec, y_spec, _resident(wo.shape), mod_spec, _resident((1, d)), _resident(wg.shape),
                  _resident(wu.shape), _resident(wd.shape)],
        out_specs=x_spec,
        out_shape=jax.ShapeDtypeStruct((t, d), F32),
        compiler_params=_params("arbitrary"),
        name="proj_ffn",
    )(x, y, wo, mod, g, wg, wu, wd)


def _ssd_in_kernel(x_ref, mod_ref, g_ref, wz_ref, wxbc_ref, wdt_ref, z_ref, xbc_ref, dt_ref):
    h = _modulated(x_ref[...], g_ref[...], mod_ref, 1).astype(BF16)
    z_ref[...] = _dot(h, wz_ref[...]).astype(BF16)
    xbc_ref[...] = _dot(h, wxbc_ref[...]).astype(BF16)
    dt_ref[...] = _dot(h, wdt_ref[...])


def _ssd_in_call(x, mod, g, wz, wxbc, wdt, *, seq):
    t, d = x.shape
    tm = ROW_TILE
    x_spec, mod_spec = _row_specs(t, d, seq, tm)
    widths = (wz.shape[1], wxbc.shape[1], wdt.shape[1])
    return pl.pallas_call(
        _ssd_in_kernel,
        grid=(t // tm,),
        in_specs=[x_spec, mod_spec, _resident((1, d)), _resident(wz.shape), _resident(wxbc.shape),
                  _resident(wdt.shape)],
        out_specs=[pl.BlockSpec((tm, w), lambda i: (i, 0)) for w in widths],
        out_shape=[jax.ShapeDtypeStruct((t, widths[0]), BF16),
                   jax.ShapeDtypeStruct((t, widths[1]), BF16),
                   jax.ShapeDtypeStruct((t, widths[2]), F32)],
        compiler_params=_params("arbitrary"),
        name="ssd_in",
    )(x, mod, g, wz, wxbc, wdt)


def _split3(v):
    v1 = v.astype(BF16)
    rem = v - v1.astype(F32)
    v2 = rem.astype(BF16)
    return v1, v2, (rem - v2.astype(F32)).astype(BF16)


def _ssd_dt_kernel(dtr_ref, dtb_ref, alog_ref, cum_ref, dtt_ref, cumt_ref):
    seq = dtr_ref.shape[0]
    q = SSD_CHUNK
    row_i = lax.broadcasted_iota(jnp.int32, (q, q), 0)
    col_j = lax.broadcasted_iota(jnp.int32, (q, q), 1)
    tri = jnp.where(row_i >= col_j, 1.0, 0.0).astype(BF16)
    lane = lax.broadcasted_iota(jnp.int32, (q, LANES), 1)
    is_fwd = (lane // SSD_HEADS_PER_GROUP) % 2 == 0
    a_coef = -jnp.exp(alog_ref[...]) * LOG2_E

    def chunk(c, carry):
        r0 = pl.multiple_of(c * q, q)
        dt = jax.nn.softplus(dtr_ref[pl.ds(r0, q), :] + dtb_ref[...])
        a = dt * a_coef
        a1, a2, a3 = _split3(a)
        prefix = _dot(tri, a1) + _dot(tri, a2) + _dot(tri, a3)
        cum = jnp.where(is_fwd, prefix, prefix[q - 1:q, :] - prefix + a)
        cum_ref[pl.ds(r0, q), :] = cum
        dtt_ref[:, pl.ds(r0, q)] = dt.T
        cumt_ref[:, pl.ds(r0, q)] = cum.T
        return carry

    lax.fori_loop(0, seq // q, chunk, 0, unroll=SSD_UNROLL)


def _ssd_dt_call(dt_raw, dt_bias, a_log, *, batch, seq):
    t = dt_raw.shape[0]
    head_major = pl.BlockSpec((None, LANES, seq), lambda b: (b, 0, 0))
    return pl.pallas_call(
        _ssd_dt_kernel,
        grid=(batch,),
        in_specs=[pl.BlockSpec((seq, LANES), lambda b: (b, 0)),
                  pl.BlockSpec((1, LANES), lambda b: (0, 0)),
                  pl.BlockSpec((1, LANES), lambda b: (0, 0))],
        out_specs=[pl.BlockSpec((seq, LANES), lambda b: (b, 0)), head_major, head_major],
        out_shape=[jax.ShapeDtypeStruct((t, LANES), F32),
                   jax.ShapeDtypeStruct((batch, LANES, seq), F32),
                   jax.ShapeDtypeStruct((batch, LANES, seq), F32)],
        compiler_params=_params("arbitrary"),
        name="ssd_dt",
    )(dt_raw, dt_bias, a_log)


def _ssd_core_kernel(xr_ref, br_ref, cr_ref, z_ref, cum_ref, dtt_ref, cumt_ref, cw_ref, cb_ref,
                     dsk_ref, ng_ref, o_ref, xt_ref, cs_ref, yt_ref, st_ref, hin_ref, dec_ref):
    seq = xr_ref.shape[0]
    q = SSD_CHUNK
    halo = BF16_SUBLANES
    nc = seq // q
    gl = GROUP_LANES
    n = SSD_STATE
    hp = SSD_HEAD_DIM
    heads = SSD_HEADS_PER_GROUP
    centre = SSD_CONV // 2
    group = pl.program_id(1)

    col_shift = jnp.where(group == 0, 0, LANES - 2 * heads * group)
    src_j = lax.broadcasted_iota(jnp.int32, (q, q), 0)
    dst_i = lax.broadcasted_iota(jnp.int32, (q, q), 1)
    causal = (dst_i >= src_j, dst_i <= src_j)
    def shift_block(k):
        rows = q if k == centre else q + halo
        first = (halo if k < centre else 0) + k - centre
        out_row = lax.broadcasted_iota(jnp.int32, (q, rows), 0)
        win_row = lax.broadcasted_iota(jnp.int32, (q, rows), 1)
        return jnp.where(win_row == out_row + first, 1.0, 0.0).astype(BF16)

    shift_all = jnp.concatenate([shift_block(k) for k in range(SSD_CONV)], axis=1)
    cw_bf = cw_ref[...].astype(BF16)

    def lanes_of(r0, rows):
        return jnp.concatenate([xr_ref[pl.ds(r0, rows), :], br_ref[pl.ds(r0, rows), :],
                                cr_ref[pl.ds(r0, rows), :]], axis=1)

    def head_rows(vals, direction):
        return jnp.concatenate(
            [jnp.broadcast_to(vals[heads * direction + h:heads * direction + h + 1, :], (hp, vals.shape[1]))
             for h in range(heads)], axis=0)

    def conv_part(c):
        r0 = pl.multiple_of(c * q, q)
        rp = pl.multiple_of(jnp.maximum(r0 - halo, 0), halo)
        rn = pl.multiple_of(jnp.minimum(r0 + q, seq - halo), halo)
        cur = lanes_of(r0, q)
        prev = lanes_of(rp, halo)
        nxt = lanes_of(rn, halo)
        prev = jnp.where(c == 0, jnp.zeros_like(prev), prev)
        nxt = jnp.where(c == nc - 1, jnp.zeros_like(nxt), nxt)
        weighted = []
        for k in range(SSD_CONV):
            w_k = cw_bf[k:k + 1, :]
            if k < centre:
                weighted += [prev * w_k, cur * w_k]
            elif k > centre:
                weighted += [cur * w_k, nxt * w_k]
            else:
                weighted.append(cur * w_k)
        act = _silu(cb_ref[...] + _dot(shift_all, jnp.concatenate(weighted, axis=0)))
        x = act[:, :gl]
        xt = jnp.concatenate([x[:, :LANES].T, x[:, LANES:].T], axis=0)
        b_bf = act[:, gl:gl + n].astype(BF16)
        c_bf = act[:, gl + n:].astype(BF16)
        xt_ref[c] = xt
        cs_ref[pl.ds(r0, q), :] = c_bf
        return xt, b_bf, c_bf

    def local_part(c, xt, b_bf, cbt):
        r0 = pl.multiple_of(c * q, q)
        cum_c = pltpu.roll(cum_ref[pl.ds(r0, q), :], col_shift, 1)
        dt_r = dtt_ref[:, pl.ds(r0, q)]
        cum_r = cumt_ref[:, pl.ds(r0, q)]
        total = jnp.concatenate([cum_r[:heads, q - 1:q], cum_r[heads:, 0:1]], axis=0)
        to_edge = jnp.exp2(total - cum_r)
        dec_ref[c] = jnp.broadcast_to(jnp.exp2(total), (2 * heads, n))
        xdt = [xt[(k % heads) * hp:(k % heads + 1) * hp, :] * dt_r[k:k + 1, :] for k in range(2 * heads)]
        st_ref[c] = _dot(jnp.concatenate([(xdt[k] * to_edge[k:k + 1, :]).astype(BF16)
                                          for k in range(2 * heads)], axis=0), b_bf)
        y_heads = []
        for h in range(heads):
            ks = (h, heads + h)
            lhs = jnp.concatenate([xdt[k].astype(BF16) for k in ks], axis=1)
            rhs = jnp.concatenate(
                [(cbt * jnp.exp2(jnp.where(causal[k // heads], cum_r[k:k + 1, :] - cum_c[:, k:k + 1],
                                           -jnp.inf))).astype(BF16) for k in ks], axis=0)
            y_heads.append(_dot(lhs, rhs))
        yt_ref[c] = jnp.concatenate(y_heads, axis=0)

    def phase_a(c, conv_c):
        xt, b_bf, c_bf = conv_c
        cbt = _dot_nt(b_bf, c_bf)
        conv_next = conv_part(jnp.minimum(c + 1, nc - 1))
        local_part(c, xt, b_bf, cbt)
        return conv_next

    lax.fori_loop(0, nc, phase_a, conv_part(0), unroll=SSD_UNROLL)

    def step(t, carry):
        new = []
        for d in range(2):
            c = t if d == 0 else nc - 1 - t
            own = st_ref[c, d * gl:(d + 1) * gl]
            hin_ref[c, d * gl:(d + 1) * gl] = carry[d].astype(BF16)
            new.append(carry[d] * head_rows(dec_ref[c], d) + own)
        return tuple(new)

    lax.fori_loop(0, nc, step, (jnp.zeros((gl, n), F32), jnp.zeros((gl, n), F32)))

    def phase_c(c, carry):
        r0 = pl.multiple_of(c * q, q)
        grow = jnp.exp2(cumt_ref[:, pl.ds(r0, q)])
        c_bf = cs_ref[pl.ds(r0, q), :]
        off = _dot_nt(hin_ref[c], c_bf)
        yt = yt_ref[c] + xt_ref[c] * dsk_ref[...]
        for d in range(2):
            yt = yt + off[d * gl:(d + 1) * gl] * head_rows(grow, d)
        y = jnp.concatenate([yt[:LANES].T, yt[LANES:].T], axis=1)
        y = y * _silu(z_ref[pl.ds(r0, q), :].astype(F32))
        o_ref[pl.ds(r0, q), :] = (_rms(y) * ng_ref[...]).astype(BF16)
        return carry

    lax.fori_loop(0, nc, phase_c, 0, unroll=SSD_UNROLL)


def _ssd_core_call(z, xbc, cum, dtt, cumt, conv_w_g, conv_b_g, d_skip_col, norm_g, *, batch, seq):
    t, d_inner = z.shape
    n = SSD_STATE
    gl = GROUP_LANES
    x_blocks = d_inner // n
    c_first = x_blocks + SSD_GROUPS
    nc = seq // SSD_CHUNK
    dir_heads = 2 * SSD_HEADS_PER_GROUP
    conv_lanes = conv_w_g.shape[-1]
    in_specs = [
        pl.BlockSpec((seq, gl), lambda b, g: (b, g)),
        pl.BlockSpec((seq, n), lambda b, g: (b, x_blocks + g)),
        pl.BlockSpec((seq, n), lambda b, g: (b, c_first + g)),
        pl.BlockSpec((seq, gl), lambda b, g: (b, g)),
        pl.BlockSpec((seq, LANES), lambda b, g: (b, 0)),
        pl.BlockSpec((None, dir_heads, seq), lambda b, g: (b, g, 0)),
        pl.BlockSpec((None, dir_heads, seq), lambda b, g: (b, g, 0)),
        pl.BlockSpec((None, SSD_CONV, conv_lanes), lambda b, g: (g, 0, 0)),
        pl.BlockSpec((None, 1, conv_lanes), lambda b, g: (g, 0, 0)),
        pl.BlockSpec((gl, LANES), lambda b, g: (g, 0)),
        pl.BlockSpec((1, gl), lambda b, g: (0, g)),
    ]
    return pl.pallas_call(
        _ssd_core_kernel,
        grid=(batch, SSD_GROUPS),
        in_specs=in_specs,
        out_specs=pl.BlockSpec((seq, gl), lambda b, g: (b, g)),
        out_shape=jax.ShapeDtypeStruct((t, d_inner), BF16),
        scratch_shapes=[
            pltpu.VMEM((nc, gl, SSD_CHUNK), F32),
            pltpu.VMEM((seq, n), BF16),
            pltpu.VMEM((nc, gl, SSD_CHUNK), F32),
            pltpu.VMEM((nc, 2 * gl, n), F32),
            pltpu.VMEM((nc, 2 * gl, n), BF16),
            pltpu.VMEM((nc, dir_heads, n), F32),
        ],
        compiler_params=_params("arbitrary", "arbitrary"),
        name="ssd_core",
    )(xbc, xbc, xbc, z, cum, dtt, cumt, conv_w_g, conv_b_g, d_skip_col, norm_g)


def _mla_in_kernel(x_ref, pos_ref, mod_ref, g_ref, win_ref, qng_ref, kvng_ref, wuq_ref, wk_ref, wv_ref,
                   qhg_ref, khg_ref, freq_ref, vone_ref, q_ref, k_ref, v_ref):
    hl = MLA_HEAD_LANES
    tm = x_ref.shape[0]
    half = MLA_ROPE // 2
    h = _modulated(x_ref[...], g_ref[...], mod_ref, 1).astype(BF16)
    lat = _dot(h, win_ref[...])
    q_lat = (_rms(lat[:, :MLA_Q_LORA]) * qng_ref[...]).astype(BF16)
    kv_lat = (_rms(lat[:, MLA_Q_LORA:MLA_Q_LORA + MLA_KV_LORA]) * kvng_ref[...]).astype(BF16)
    k_pe = lat[:, MLA_Q_LORA + MLA_KV_LORA:]
    qf = _dot(q_lat, wuq_ref[...])
    kf = _dot(kv_lat, wk_ref[...])
    v_ref[...] = (_dot(kv_lat, wv_ref[...]) + vone_ref[...]).astype(BF16)

    ang = freq_ref[...] * pos_ref[...]
    cos_f, sin_f = jnp.cos(ang), jnp.sin(ang)
    rest = hl // 2 - half
    cos_t = jnp.concatenate([cos_f, jnp.ones((rest, tm), F32)] * 2, axis=0)
    sin_t = jnp.concatenate([-sin_f, jnp.zeros((rest, tm), F32), sin_f, jnp.zeros((rest, tm), F32)], axis=0)

    def token_major(tab):
        return jnp.concatenate([tab[:, r:r + LANES].T for r in range(0, tm, LANES)], axis=0)

    cos = jnp.concatenate([token_major(cos_t)] * 2, axis=1)
    sin = jnp.concatenate([token_major(sin_t)] * 2, axis=1)
    q_scale = MLA_QK ** -0.5 * LOG2_E
    blk_r = lax.broadcasted_iota(jnp.int32, (2 * hl, 2 * hl), 0) // hl
    blk_c = lax.broadcasted_iota(jnp.int32, (2 * hl, 2 * hl), 1) // hl
    head_ones = jnp.where(blk_r == blk_c, 1.0, 0.0).astype(BF16)

    def pair_norm_rope(t, gain):
        sq = _dot((t * t).astype(BF16), head_ones)
        t = t * lax.rsqrt(sq + MLA_QK * NORM_EPS) * gain
        partner = jnp.concatenate([pltpu.roll(t[:, :hl], hl // 2, 1), pltpu.roll(t[:, hl:], hl // 2, 1)], axis=1)
        return t * cos + partner * sin

    q_gain = jnp.concatenate([qhg_ref[...]] * 2, axis=1) * (MLA_QK ** 0.5 * q_scale)
    k_gain = jnp.concatenate([khg_ref[...]] * 2, axis=1) * (MLA_QK ** 0.5)
    k_pe2 = jnp.concatenate([k_pe] * 2, axis=1)
    for pr in range(MLA_HEADS // 2):
        sl = slice(2 * pr * hl, 2 * (pr + 1) * hl)
        q_ref[:, sl] = pair_norm_rope(qf[:, sl], q_gain).astype(BF16)
        k_ref[:, sl] = pair_norm_rope(kf[:, sl] + k_pe2, k_gain).astype(BF16)


def _mla_in_call(x, pos, mod, g, win, qng, kvng, wuq, wk, wv, qhg, khg, freq, vone, *, seq):
    t, d = x.shape
    tm = ROW_TILE
    x_spec, mod_spec = _row_specs(t, d, seq, tm)
    qk_w = MLA_HEADS * MLA_HEAD_LANES
    v_w = qk_w
    return pl.pallas_call(
        _mla_in_kernel,
        grid=(t // tm,),
        in_specs=[x_spec, pl.BlockSpec((None, 1, tm), lambda i: (i, 0, 0)), mod_spec, _resident((1, d)),
                  _resident(win.shape), _resident(qng.shape), _resident(kvng.shape),
                  _resident(wuq.shape), _resident(wk.shape), _resident(wv.shape),
                  _resident(qhg.shape), _resident(khg.shape), _resident(freq.shape),
                  _resident(vone.shape)],
        out_specs=[pl.BlockSpec((tm, qk_w), lambda i: (i, 0)),
                   pl.BlockSpec((tm, qk_w), lambda i: (i, 0)),
                   pl.BlockSpec((tm, v_w), lambda i: (i, 0))],
        out_shape=[jax.ShapeDtypeStruct((t, qk_w), BF16),
                   jax.ShapeDtypeStruct((t, qk_w), BF16),
                   jax.ShapeDtypeStruct((t, v_w), BF16)],
        compiler_params=_params("arbitrary"),
        name="mla_in",
    )(x, pos.reshape(t // tm, 1, tm), mod, g, win, qng, kvng, wuq, wk, wv, qhg, khg, freq, vone)


def _attn_kernel(q_ref, k_ref, v_ref, o_ref):
    hl = MLA_HEAD_LANES
    rows = q_ref.shape[0] // ATTN_ROW_SPLIT
    lane = lax.broadcasted_iota(jnp.int32, (rows, o_ref.shape[1]), 1)
    items = [(part, hd) for part in range(ATTN_ROW_SPLIT) for hd in range(2)]
    s = [_dot_nt(q_ref[part * rows:(part + 1) * rows, hd * hl:(hd + 1) * hl],
                 k_ref[:, hd * hl:(hd + 1) * hl]) for part, hd in items]
    outs = []
    for idx, (part, hd) in enumerate(items):
        p = jnp.exp2(s[idx] - jnp.max(s[idx], axis=-1, keepdims=True))
        r = _dot(p.astype(BF16), v_ref[:, hd * hl:(hd + 1) * hl])
        outs.append(r / r[:, MLA_V:MLA_V + 1])
    for part in range(ATTN_ROW_SPLIT):
        second = pltpu.roll(outs[2 * part + 1], MLA_V, 1)
        out = jnp.where(lane >= MLA_V, second, outs[2 * part])
        o_ref[part * rows:(part + 1) * rows, :] = out.astype(BF16)


def _attn_call(q, k, v, *, batch, seq):
    tq = ATTN_Q_TILE
    nq = seq // tq
    pair_qk = 2 * MLA_HEAD_LANES
    pair_v = 2 * MLA_V
    return pl.pallas_call(
        _attn_kernel,
        grid=(batch, MLA_HEADS // 2, nq),
        in_specs=[pl.BlockSpec((tq, pair_qk), lambda b, p, i: (b * nq + i, p)),
                  pl.BlockSpec((seq, pair_qk), lambda b, p, i: (b, p)),
                  pl.BlockSpec((seq, pair_qk), lambda b, p, i: (b, p))],
        out_specs=pl.BlockSpec((tq, pair_v), lambda b, p, i: (b * nq + i, p)),
        out_shape=jax.ShapeDtypeStruct((q.shape[0], MLA_HEADS * MLA_V), BF16),
        compiler_params=_params("arbitrary", "arbitrary", "arbitrary"),
        name="attn",
    )(q, k, v)


def _mla_head_perm():
    half = MLA_ROPE // 2
    perm = [-1] * MLA_HEAD_LANES
    for l in range(half):
        perm[l] = MLA_NOPE + l
        perm[MLA_HEAD_LANES // 2 + l] = MLA_NOPE + half + l
    nope_lo = MLA_HEAD_LANES // 2 - half
    for l in range(nope_lo):
        perm[half + l] = l
    for l in range(MLA_NOPE - nope_lo):
        perm[MLA_HEAD_LANES // 2 + half + l] = nope_lo + l
    return perm


def _gather_cols(w, perm):
    idx = jnp.array([max(p, 0) for p in perm], jnp.int32)
    keep = jnp.array([p >= 0 for p in perm])
    return jnp.where(keep, jnp.take(w, idx, axis=-1), 0.0)


def _mla_layout(w_in, w_uq, w_ukv, q_head_g, k_head_g):
    perm = _mla_head_perm()
    lat = MLA_Q_LORA + MLA_KV_LORA
    pe_perm = [p - MLA_NOPE if p >= MLA_NOPE else -1 for p in perm]
    win = jnp.concatenate([w_in[:, :lat], _gather_cols(w_in[:, lat:], pe_perm)], axis=1)
    wuq = _gather_cols(w_uq.reshape(MLA_Q_LORA, MLA_HEADS, MLA_QK), perm)
    wuq = wuq.reshape(MLA_Q_LORA, MLA_HEADS * MLA_HEAD_LANES)
    kv = w_ukv.reshape(MLA_KV_LORA, MLA_HEADS, MLA_NOPE + MLA_V)
    nope_perm = [p if 0 <= p < MLA_NOPE else -1 for p in perm]
    wk = _gather_cols(kv[:, :, :MLA_NOPE], nope_perm).reshape(MLA_KV_LORA, MLA_HEADS * MLA_HEAD_LANES)
    v_pad = [(0, 0), (0, 0), (0, MLA_HEAD_LANES - MLA_V)]
    wv = jnp.pad(kv[:, :, MLA_NOPE:], v_pad).reshape(MLA_KV_LORA, MLA_HEADS * MLA_HEAD_LANES)
    vone = jnp.tile((jnp.arange(MLA_HEAD_LANES) == MLA_V).astype(F32), MLA_HEADS)[None, :]
    qhg = _gather_cols(q_head_g[None, :], perm)
    khg = _gather_cols(k_head_g[None, :], perm)
    freq = (ROPE_THETA ** (-jnp.arange(0, MLA_ROPE, 2, dtype=F32) / MLA_ROPE))[:, None]
    return (win.astype(BF16), wuq.astype(BF16), wk.astype(BF16), wv.astype(BF16), qhg, khg, freq, vone)


def _ssd_layout(w_in, conv_w, conv_b, dt_bias, a_log, d_skip):
    d_inner = SSD_GROUPS * GROUP_LANES
    bc_ch = SSD_GROUPS * SSD_STATE
    conv_ch = d_inner + 2 * bc_ch
    heads = SSD_GROUPS * SSD_HEADS_PER_GROUP
    wz = w_in[:, :d_inner].astype(BF16)
    wxbc = w_in[:, d_inner:d_inner + conv_ch].astype(BF16)

    def conv_by_group(v):
        rows = v.shape[0]
        parts = [v[:, :d_inner].reshape(rows, SSD_GROUPS, GROUP_LANES),
                 v[:, d_inner:d_inner + bc_ch].reshape(rows, SSD_GROUPS, SSD_STATE),
                 v[:, d_inner + bc_ch:].reshape(rows, SSD_GROUPS, SSD_STATE)]
        return jnp.swapaxes(jnp.concatenate(parts, axis=-1), 0, 1)

    def by_group(v):
        lead = v.shape[:-1]
        v = v.reshape(lead + (2, SSD_GROUPS, SSD_HEADS_PER_GROUP))
        v = jnp.swapaxes(v, -3, -2).reshape(lead + (2 * heads,))
        return jnp.pad(v, [(0, 0)] * len(lead) + [(0, LANES - 2 * heads)])

    wdt = by_group(w_in[:, d_inner + conv_ch:]).astype(BF16)
    dtb = by_group(dt_bias.reshape(1, 2 * heads))
    a_log_cols = by_group(a_log.reshape(1, 2 * heads))
    d_skip_col = jnp.broadcast_to(jnp.repeat(d_skip, SSD_HEAD_DIM)[:, None], (d_inner, LANES))
    return wz, wxbc, wdt, dtb, a_log_cols, d_skip_col, conv_by_group(conv_w), conv_by_group(conv_b[None, :])


def kernel(x, c, positions, norm_g, w_mod, b_mod, ffn_w_gate, ffn_w_up, ffn_w_down, ssd_w_in, ssd_conv_w,
           ssd_conv_b, ssd_dt_bias, ssd_a_log, ssd_d, ssd_norm_g, ssd_w_out, mla_w_in, mla_q_norm_g,
           mla_kv_norm_g, mla_w_uq, mla_w_ukv, mla_q_head_g, mla_k_head_g, mla_w_out):
    batch, seq, d = x.shape
    depth = w_mod.shape[0]
    t = batch * seq
    n_mixers = 2
    xf = x.reshape(t, d)
    pos = positions.astype(F32).reshape(t, 1)
    mod = _mod_call(c, w_mod, b_mod).reshape(depth, batch, 3 * N_SUBLAYERS, d)
    wg, wu, wd = (w.astype(BF16) for w in (ffn_w_gate, ffn_w_up, ffn_w_down))

    for i in range(depth):
        g = norm_g[i][:, None, :]
        xf = _ffn_call(xf, mod[i], g[0], wg[i, 0], wu[i, 0], wd[i, 0], sub=0, seq=seq)
        j = i // n_mixers
        if i % n_mixers == 0:
            wz, wxbc, wdt, dtb, a_log_cols, d_skip_col, conv_w_g, conv_b_g = _ssd_layout(
                ssd_w_in[j], ssd_conv_w[j], ssd_conv_b[j], ssd_dt_bias[j], ssd_a_log[j], ssd_d[j])
            z, xbc, dt_raw = _ssd_in_call(xf, mod[i], g[1], wz, wxbc, wdt, seq=seq)
            cum, dtt, cumt = _ssd_dt_call(dt_raw, dtb, a_log_cols, batch=batch, seq=seq)
            y = _ssd_core_call(z, xbc, cum, dtt, cumt, conv_w_g, conv_b_g, d_skip_col,
                               ssd_norm_g[j][None, :], batch=batch, seq=seq)
            w_out = ssd_w_out[j].astype(BF16)
        else:
            win, wuq, wk, wv, qhg, khg, freq, vone = _mla_layout(
                mla_w_in[j], mla_w_uq[j], mla_w_ukv[j], mla_q_head_g[j], mla_k_head_g[j])
            q, k, v = _mla_in_call(xf, pos, mod[i], g[1], win, mla_q_norm_g[j][None, :],
                                   mla_kv_norm_g[j][None, :], wuq, wk, wv, qhg, khg, freq, vone, seq=seq)
            y = _attn_call(q, k, v, batch=batch, seq=seq)
            w_out = mla_w_out[j].astype(BF16)
        xf = _proj_ffn_call(xf, y, w_out, mod[i], g[2], wg[i, 1], wu[i, 1], wd[i, 1], seq=seq)
    return xf.reshape(batch, seq, d)
```
